```python
import jax, jax.numpy as jnp
from jax import lax
import numpy as np

D_MODEL = 1024
BATCH = 8
SEQ = 2048
DEPTH = 4

CTX_LEN = 256
GRID_W = 64
N_MIXERS = 3
EPS = 1e-6

A_HEADS = 8
A_DK = 64
A_DV = 128
A_QK = A_HEADS * A_DK
A_V = A_HEADS * A_DV
A_IN = 2 * A_QK + 2 * A_V + 4 * A_HEADS
A_CHUNK = 128
A_CONV = 3
A_FORGET_BIAS = 3.0

B_HEADS = 4
B_DK = D_MODEL // 2 // B_HEADS
B_DV = D_MODEL // B_HEADS
B_QK = B_HEADS * B_DK
B_V = B_HEADS * B_DV
B_RANK = 16
B_TAU = 16.0
B_IN = 2 * B_QK + 2 * B_V + 2 * B_RANK
B_CHUNK = 64

C_HEADS = 16
C_DH = D_MODEL // C_HEADS
C_WIN_ROWS = 8
C_WIN_COLS = 16
C_QBLOCK = 16
C_KSPAN = C_QBLOCK + C_WIN_COLS

FFN_DIM = 2816
FFN_CONV = 3

N_A = (DEPTH + 2) // 3
N_B = (DEPTH + 1) // 3
N_C = DEPTH // 3

kernel_name = "hybrid_mlstm_gla_natten_dit"


def rmsnorm(x, g):
    x32 = x.astype(jnp.float32)
    y = x32 * lax.rsqrt(jnp.mean(x32 * x32, axis=-1, keepdims=True) + EPS)
    return (y * g.astype(jnp.float32)).astype(x.dtype)


def dwconv3(x, w, b):
    xp = jnp.pad(x, ((0, 0), (1, 1), (0, 0)))
    return xp[:, :-2] * w[0] + xp[:, 1:-1] * w[1] + xp[:, 2:] * w[2] + b


def to_heads(x, h):
    b, t, _ = x.shape
    return x.reshape(b, t, h, -1).transpose(0, 2, 1, 3)


def from_heads(x):
    b, h, t, d = x.shape
    return x.transpose(0, 2, 1, 3).reshape(b, t, h * d)


def chunk_seq(x, L):
    b, h, t = x.shape[:3]
    x = x.reshape(b, h, t // L, L, *x.shape[3:])
    return jnp.moveaxis(x, 2, 0)


def unchunk(x):
    x = jnp.moveaxis(x, 0, 2)
    return x.reshape(x.shape[0], x.shape[1], -1, *x.shape[4:])


def mlstm_scan(q, k, v, ig, lf, state):
    L = A_CHUNK
    causal = jnp.tril(jnp.ones((L, L), bool))

    def step(carry, inp):
        C, n, m = carry
        qc, kc, vc, ic, fc = inp
        g = jnp.cumsum(fc, axis=-1)
        logD = jnp.where(causal, g[..., :, None] - g[..., None, :] + ic[..., None, :], -jnp.inf)
        m_inter = g + m[..., None]
        m_row = jnp.maximum(jnp.max(logD, axis=-1), m_inter)
        S = jnp.einsum('bhjd,bhsd->bhjs', qc, kc) * jnp.exp(logD - m_row[..., None])
        inter = jnp.exp(m_inter - m_row)
        num = jnp.einsum('bhjs,bhse->bhje', S, vc) + inter[..., None] * jnp.einsum('bhjd,bhde->bhje', qc, C)
        den = jnp.sum(S, axis=-1) + inter * jnp.einsum('bhjd,bhd->bhj', qc, n)
        h = num / jnp.maximum(jnp.abs(den), jnp.exp(-m_row))[..., None]
        gL = g[..., -1]
        logw = gL[..., None] - g + ic
        m_new = jnp.maximum(gL + m, jnp.max(logw, axis=-1))
        w = jnp.exp(logw - m_new[..., None])
        decay = jnp.exp(gL + m - m_new)
        C_new = decay[..., None, None] * C + jnp.einsum('bhs,bhsd,bhse->bhde', w, kc, vc)
        n_new = decay[..., None] * n + jnp.einsum('bhs,bhsd->bhd', w, kc)
        return (C_new, n_new, m_new), h

    xs = tuple(chunk_seq(t, L) for t in (q, k, v, ig, lf))
    state, h = lax.scan(step, state, xs)
    return unchunk(h), state


def gla_scan(q, k, v, g, S):
    L = B_CHUNK
    causal = jnp.tril(jnp.ones((L, L), bool))[..., None]

    def step(S, inp):
        qc, kc, vc, gc = inp
        G = jnp.cumsum(gc, axis=2)
        Dm = jnp.exp(jnp.where(causal, G[:, :, :, None, :] - G[:, :, None, :, :], -jnp.inf))
        A = jnp.einsum('bhjd,bhsd,bhjsd->bhjs', qc, kc, Dm)
        o = jnp.einsum('bhjs,bhse->bhje', A, vc) + jnp.einsum('bhjd,bhde->bhje', qc * jnp.exp(G), S)
        GL = G[:, :, -1]
        S_new = jnp.exp(GL)[..., None] * S + jnp.einsum('bhsd,bhse->bhde', kc * jnp.exp(GL[:, :, None] - G), vc)
        return S_new, o

    xs = tuple(chunk_seq(t, L) for t in (q, k, v, g))
    S, o = lax.scan(step, S, xs)
    return unchunk(o), S


def bidir_scan(scan_fn, ctx_f, lat_f, ctx_b, lat_b, state0):
    rev = lambda xs: tuple(jnp.flip(t, 2) for t in xs)
    hc_f, st = scan_fn(*ctx_f, state0)
    hl_f, _ = scan_fn(*lat_f, st)
    hc_b, st = scan_fn(*rev(ctx_b), state0)
    hl_b, _ = scan_fn(*rev(lat_b), st)
    return hc_f + jnp.flip(hc_b, 2), hl_f + jnp.flip(hl_b, 2)


def mixer_mlstm(h_ctx, h_lat, w_in, b_gate, conv_w, conv_b, w_out, ctx_out):
    def project(h):
        p = h @ w_in
        qk, v, o, gates = jnp.split(p, [2 * A_QK, 2 * A_QK + A_V, 2 * A_QK + 2 * A_V], axis=-1)
        qk = jax.nn.silu(dwconv3(qk, conv_w, conv_b))
        q, k = jnp.split(qk, 2, axis=-1)
        q = to_heads(q, A_HEADS).astype(jnp.float32) * (A_DK ** -0.5)
        k = to_heads(k, A_HEADS).astype(jnp.float32)
        v = to_heads(v, A_HEADS).astype(jnp.float32)
        b, t, _ = gates.shape
        g = (gates + b_gate).astype(jnp.float32).reshape(b, t, 4, A_HEADS).transpose(2, 0, 3, 1)
        fwd = (q, k, v, g[0], jax.nn.log_sigmoid(g[1]))
        bwd = (q, k, v, g[2], jax.nn.log_sigmoid(g[3]))
        return fwd, bwd, o

    cf, cb, o_c = project(h_ctx)
    lf, lb, o_l = project(h_lat)
    bn = h_lat.shape[0]
    state0 = (jnp.zeros((bn, A_HEADS, A_DK, A_DV), jnp.float32),
              jnp.zeros((bn, A_HEADS, A_DK), jnp.float32),
              jnp.zeros((bn, A_HEADS), jnp.float32))
    hc, hl = bidir_scan(mlstm_scan, cf, lf, cb, lb, state0)

    def out(hh, o):
        return (from_heads(hh).astype(o.dtype) * jax.nn.sigmoid(o)) @ w_out

    return (out(hc, o_c) if ctx_out else None), out(hl, o_l)


def mixer_gla(h_ctx, h_lat, w_in, w_a2, b_a, norm_g, w_out, ctx_out):
    def project(h):
        p = h @ w_in
        q, k, v, r, a = jnp.split(p, [B_QK, 2 * B_QK, 2 * B_QK + B_V, 2 * B_QK + 2 * B_V], axis=-1)
        q = to_heads(q, B_HEADS).astype(jnp.float32) * (B_DK ** -0.5)
        k = to_heads(k, B_HEADS).astype(jnp.float32)
        v = to_heads(v, B_HEADS).astype(jnp.float32)
        a = a.astype(jnp.float32)

        def decay(d):
            z = a[..., d * B_RANK:(d + 1) * B_RANK] @ w_a2[d].astype(jnp.float32) + b_a[d].astype(jnp.float32)
            return to_heads(jax.nn.log_sigmoid(z) / B_TAU, B_HEADS)

        return (q, k, v, decay(0)), (q, k, v, decay(1)), r

    cf, cb, r_c = project(h_ctx)
    lf, lb, r_l = project(h_lat)
    bn = h_lat.shape[0]
    state0 = jnp.zeros((bn, B_HEADS, B_DK, B_DV), jnp.float32)
    oc, ol = bidir_scan(gla_scan, cf, lf, cb, lb, state0)

    def out(oo, r):
        oo = oo * lax.rsqrt(jnp.mean(oo * oo, axis=-1, keepdims=True) + EPS) * norm_g.astype(jnp.float32)
        return (from_heads(oo).astype(r.dtype) * jax.nn.silu(r)) @ w_out

    return (out(oc, r_c) if ctx_out else None), out(ol, r_l)


def mixer_na(h_ctx, h_lat, w_in, rpb, w_out, ctx_out):
    bn, T, _ = h_lat.shape
    rows = T // GRID_W
    kr = min(C_WIN_ROWS, rows)
    scale = C_DH ** -0.5
    q_l, k_l, v_l = [t.reshape(bn, T, C_HEADS, C_DH) for t in jnp.split(h_lat @ w_in, 3, axis=-1)]
    q_c, k_c, v_c = [t.reshape(bn, -1, C_HEADS, C_DH) for t in jnp.split(h_ctx @ w_in, 3, axis=-1)]
    q_l = q_l * scale
    q_c = q_c * scale

    n_cb = GRID_W // C_QBLOCK
    col_q = np.arange(GRID_W).reshape(n_cb, C_QBLOCK)
    k_start = np.clip(np.arange(n_cb) * C_QBLOCK - C_WIN_COLS // 2, 0, GRID_W - C_KSPAN)
    col_k = k_start[:, None] + np.arange(C_KSPAN)
    c_start = np.clip(col_q - C_WIN_COLS // 2, 0, GRID_W - C_WIN_COLS)
    col_valid = (col_k[:, None, :] >= c_start[:, :, None]) & (col_k[:, None, :] < c_start[:, :, None] + C_WIN_COLS)
    dc = np.clip(col_k[:, None, :] - col_q[:, :, None] + C_WIN_COLS - 1, 0, 2 * C_WIN_COLS - 2)

    q_g = q_l.reshape(bn, rows, n_cb, C_QBLOCK, C_HEADS, C_DH)
    k_cols = k_l.reshape(bn, rows, GRID_W, C_HEADS, C_DH)[:, :, col_k]
    v_cols = v_l.reshape(bn, rows, GRID_W, C_HEADS, C_DH)[:, :, col_k]
    n_loc = kr * C_KSPAN

    def row_fn(r):
        rs = jnp.clip(r - kr // 2, 0, rows - kr)
        qr = lax.dynamic_index_in_dim(q_g, r, axis=1, keepdims=False)
        kw = lax.dynamic_slice_in_dim(k_cols, rs, kr, axis=1)
        vw = lax.dynamic_slice_in_dim(v_cols, rs, kr, axis=1)
        s_loc = jnp.einsum('bnqhd,brnkhd->bhnqrk', qr, kw).astype(jnp.float32)
        dr = rs + jnp.arange(kr) - r + (C_WIN_ROWS - 1)
        bias = rpb[:, dr[None, None, :, None], dc[:, :, None, :]].astype(jnp.float32)
        s_loc = jnp.where(col_valid[:, :, None, :], s_loc + bias, -jnp.inf)
        s_ctx = jnp.einsum('bnqhd,bjhd->bhnqj', qr, k_c).astype(jnp.float32)
        s = jnp.concatenate([s_loc.reshape(bn, C_HEADS, n_cb, C_QBLOCK, n_loc), s_ctx], axis=-1)
        p = jax.nn.softmax(s, axis=-1).astype(vw.dtype)
        p_loc = p[..., :n_loc].reshape(bn, C_HEADS, n_cb, C_QBLOCK, kr, C_KSPAN)
        o = (jnp.einsum('bhnqrk,brnkhd->bnqhd', p_loc, vw)
             + jnp.einsum('bhnqj,bjhd->bnqhd', p[..., n_loc:], v_c))
        return o.reshape(bn, GRID_W, C_HEADS, C_DH)

    o = lax.map(row_fn, jnp.arange(rows))
    y_lat = jnp.moveaxis(o, 0, 1).reshape(bn, T, D_MODEL) @ w_out
    y_ctx = None
    if ctx_out:
        s = jnp.einsum('bihd,bjhd->bhij', q_c, k_c).astype(jnp.float32)
        p = jax.nn.softmax(s, axis=-1).astype(v_c.dtype)
        y_ctx = jnp.einsum('bhij,bjhd->bihd', p, v_c).reshape(bn, -1, D_MODEL) @ w_out
    return y_ctx, y_lat


def conv_ffn(h, w_up, conv_w, conv_b, w_down):
    u = dwconv3(h @ w_up, conv_w, conv_b)
    a, g = jnp.split(u, 2, axis=-1)
    return (jax.nn.silu(g) * a) @ w_down


def setup_inputs(seed: int = 0) -> dict:
    key = jax.random.key(seed)
    ks = jax.random.split(key, 32)
    D = D_MODEL
    nrm = lambda k, shape, s: jax.random.normal(k, shape, jnp.float32) * s
    forget_off = jnp.tile(jnp.repeat(jnp.array([0.0, A_FORGET_BIAS], jnp.float32), A_HEADS), 2)
    return {
        "x": nrm(ks[0], (BATCH, SEQ, D), 1.0),
        "c": nrm(ks[1], (BATCH, D), 1.0),
        "ctx": nrm(ks[2], (BATCH, CTX_LEN, D), 1.0),
        "c_ctx": nrm(ks[3], (D,), 1.0),
        "w_ada": nrm(ks[4], (DEPTH, D, 6 * D), 0.5 * D ** -0.5),
        "b_ada": nrm(ks[5], (DEPTH, 6 * D), 0.02),
        "norm_mix": 1.0 + nrm(ks[6], (DEPTH, D), 0.02),
        "norm_ffn": 1.0 + nrm(ks[7], (DEPTH, D), 0.02),
        "w_up": nrm(ks[8], (DEPTH, D, 2 * FFN_DIM), D ** -0.5),
        "ffn_conv_w": nrm(ks[9], (DEPTH, FFN_CONV, 2 * FFN_DIM), FFN_CONV ** -0.5),
        "ffn_conv_b": nrm(ks[10], (DEPTH, 2 * FFN_DIM), 0.02),
        "w_down": nrm(ks[11], (DEPTH, FFN_DIM, D), FFN_DIM ** -0.5),
        "a_w_in": nrm(ks[12], (N_A, D, A_IN), D ** -0.5),
        "a_b_gate": nrm(ks[13], (N_A, 4 * A_HEADS), 0.1) + forget_off,
        "a_conv_w": nrm(ks[14], (N_A, A_CONV, 2 * A_QK), A_CONV ** -0.5),
        "a_conv_b": nrm(ks[15], (N_A, 2 * A_QK), 0.02),
        "a_w_out": nrm(ks[16], (N_A, A_V, D), A_V ** -0.5),
        "b_w_in": nrm(ks[17], (N_B, D, B_IN), D ** -0.5),
        "b_w_a2": nrm(ks[18], (N_B, 2, B_RANK, B_QK), B_RANK ** -0.5),
        "b_b_a": nrm(ks[19], (N_B, 2, B_QK), 0.1),
        "b_norm": 1.0 + nrm(ks[20], (N_B, B_DV), 0.02),
        "b_w_out": nrm(ks[21], (N_B, B_V, D), B_V ** -0.5),
        "c_w_in": nrm(ks[22], (N_C, D, 3 * D), D ** -0.5),
        "c_rpb": nrm(ks[23], (N_C, C_HEADS, 2 * C_WIN_ROWS - 1, 2 * C_WIN_COLS - 1), 0.1),
        "c_w_out": nrm(ks[24], (N_C, D, D), D ** -0.5),
        "norm_final": 1.0 + nrm(ks[25], (D,), 0.02),
    }


def reference(x, c, ctx, c_ctx, w_ada, b_ada, norm_mix, norm_ffn, w_up, ffn_conv_w, ffn_conv_b, w_down,
              a_w_in, a_b_gate, a_conv_w, a_conv_b, a_w_out,
              b_w_in, b_w_a2, b_b_a, b_norm, b_w_out,
              c_w_in, c_rpb, c_w_out, norm_final):
    x_lat, x_ctx = x, ctx
    s_lat = jax.nn.silu(c)
    s_ctx = jax.nn.silu(c_ctx)
    for i in range(DEPTH):
        last = i == DEPTH - 1
        m_lat = (s_lat @ w_ada[i] + b_ada[i])[:, None, :]
        m_ctx = s_ctx @ w_ada[i] + b_ada[i]
        sh1, sc1, g1, sh2, sc2, g2 = jnp.split(m_lat, 6, axis=-1)
        csh1, csc1, cg1, csh2, csc2, cg2 = jnp.split(m_ctx, 6, axis=-1)
        h_lat = rmsnorm(x_lat, norm_mix[i]) * (1.0 + sc1) + sh1
        h_ctx = rmsnorm(x_ctx, norm_mix[i]) * (1.0 + csc1) + csh1
        kind, j = i % N_MIXERS, i // N_MIXERS
        if kind == 0:
            y_ctx, y_lat = mixer_mlstm(h_ctx, h_lat, a_w_in[j], a_b_gate[j], a_conv_w[j], a_conv_b[j], a_w_out[j], not last)
        elif kind == 1:
            y_ctx, y_lat = mixer_gla(h_ctx, h_lat, b_w_in[j], b_w_a2[j], b_b_a[j], b_norm[j], b_w_out[j], not last)
        else:
            y_ctx, y_lat = mixer_na(h_ctx, h_lat, c_w_in[j], c_rpb[j], c_w_out[j], not last)
        x_lat = x_lat + g1 * y_lat
        h_lat = rmsnorm(x_lat, norm_ffn[i]) * (1.0 + sc2) + sh2
        x_lat = x_lat + g2 * conv_ffn(h_lat, w_up[i], ffn_conv_w[i], ffn_conv_b[i], w_down[i])
        if not last:
            x_ctx = x_ctx + cg1 * y_ctx
            h_ctx = rmsnorm(x_ctx, norm_ffn[i]) * (1.0 + csc2) + csh2
            x_ctx = x_ctx + cg2 * conv_ffn(h_ctx, w_up[i], ffn_conv_w[i], ffn_conv_b[i], w_down[i])
    return rmsnorm(x_lat, norm_final)
```

```python
import functools

import jax
import jax.numpy as jnp
import numpy as np
from jax import lax
from jax.experimental import pallas as pl
from jax.experimental.pallas import tpu as pltpu

F32 = jnp.float32
BF16 = jnp.bfloat16

D_MODEL = 1024
EPS = 1e-6
CTX_LEN = 256
GRID_W = 64

ROW_BLOCK = 256
HALO = 16
VMEM_LIMIT = 56 * 1024 * 1024

A_HEADS, A_DK, A_DV, A_CHUNK = 8, 64, 128, 128
A_QK, A_V = A_HEADS * A_DK, A_HEADS * A_DV
B_HEADS, B_DK, B_DV, B_CHUNK = 4, 128, 256, 64
B_QK, B_V, B_RANK, B_TAU = B_HEADS * B_DK, B_HEADS * B_DV, 16, 16.0
B_SUB = 8
C_HEADS, C_DH, C_WIN_ROWS, C_WIN_COLS = 16, 64, 8, 16
FFN_DIM = 2816
FFN_TILE = 256
GATE_PAD = 128

NT_DIMS = (((1,), (1,)), ((), ()))
TN_DIMS = (((0,), (0,)), ((), ()))


def _cparams(sem):
    return pltpu.CompilerParams(dimension_semantics=sem, vmem_limit_bytes=VMEM_LIMIT)


def _resident(shape):
    nd = len(shape)
    return pl.BlockSpec(shape, lambda *_: (0,) * nd, pipeline_mode=pl.Buffered(1))


def _dot(a, b):
    return jnp.dot(a, b, preferred_element_type=F32)


def _dot_nt(a, b):
    return lax.dot_general(a, b, NT_DIMS, preferred_element_type=F32)


def _dot_tn(a, b):
    return lax.dot_general(a, b, TN_DIMS, preferred_element_type=F32)


def _split3(x):
    x1 = x.astype(BF16)
    r1 = x - x1.astype(F32)
    x2 = r1.astype(BF16)
    x3 = (r1 - x2.astype(F32)).astype(BF16)
    return x1, x2, x3


def _cumsum_rows(mask_bf16, x):
    x1, x2, x3 = _split3(x)
    return _dot(mask_bf16, x1) + _dot(mask_bf16, x2) + _dot(mask_bf16, x3)


def _cumsum_cols(x, mask_bf16):
    x1, x2, x3 = _split3(x)
    return _dot_nt(x1, mask_bf16) + _dot_nt(x2, mask_bf16) + _dot_nt(x3, mask_bf16)


def _log_sigmoid(z):
    return jnp.minimum(z, 0.0) - jnp.log1p(jnp.exp(-jnp.abs(z)))


def _silu(z):
    return z * jax.nn.sigmoid(z)


def _rms(x, gain):
    return x * lax.rsqrt(jnp.mean(x * x, axis=-1, keepdims=True) + EPS) * gain


def _norm_mod(x, gain, scale, shift):
    return _rms(x, gain) * (1.0 + scale) + shift


def _mod_part(mod, i):
    return mod[:, i * D_MODEL:(i + 1) * D_MODEL]


def _shift_rows(u, rows):
    n = rows + 2 * HALO
    prev = pltpu.roll(u, 1, 0)[HALO:HALO + rows]
    nxt = pltpu.roll(u, n - 1, 0)[HALO:HALO + rows]
    return prev, u[HALO:HALO + rows], nxt


def _dwconv3(u, w, b, rows):
    prev, cur, nxt = _shift_rows(u, rows)
    return prev * w[0:1] + cur * w[1:2] + nxt * w[2:3] + b


def _fill_halo_block(hb_ref, xp_ref, xm_ref, xn_ref, gain, scale, shift, t, nblk, seg):
    has_prev = jnp.logical_and(t != 0, t != seg).astype(F32)
    has_next = jnp.logical_and(t != nblk - 1, t != seg - 1).astype(F32)
    hb_ref[0:HALO] = (_norm_mod(xp_ref[0], gain, scale, shift) * has_prev).astype(BF16)
    hb_ref[HALO:HALO + ROW_BLOCK] = _norm_mod(xm_ref[0], gain, scale, shift).astype(BF16)
    hb_ref[HALO + ROW_BLOCK:] = (_norm_mod(xn_ref[0], gain, scale, shift) * has_next).astype(BF16)


def _halo_specs(nblk):
    per = ROW_BLOCK // HALO
    last = nblk * per - 1
    prev = pl.BlockSpec((1, HALO, D_MODEL), lambda b, t: (b, jnp.maximum(t * per - 1, 0), 0))
    main = pl.BlockSpec((1, ROW_BLOCK, D_MODEL), lambda b, t: (b, t, 0))
    nxt = pl.BlockSpec((1, HALO, D_MODEL), lambda b, t: (b, jnp.minimum((t + 1) * per, last), 0))
    return prev, main, nxt


def _mod_spec(t0):
    return pl.BlockSpec((1, 1, 6 * D_MODEL), lambda b, t: (2 * b + jnp.minimum(t + t0, 1), 0, 0))


def _row_spec(width, t0=0):
    return pl.BlockSpec((1, ROW_BLOCK, width), lambda b, t: (b, t + t0, 0))


def _ada_kernel(c_ref, w_ref, b_ref, o_ref):
    s = _silu(c_ref[...]).astype(BF16)
    o_ref[0] = _dot(s, w_ref[0].astype(BF16)) + b_ref[0]


def _ada_modulation(cond, w_ada, b_ada):
    depth, _, n = w_ada.shape
    tn = 1024
    rows = cond.shape[0]
    return pl.pallas_call(
        _ada_kernel,
        grid=(depth, n // tn),
        in_specs=[pl.BlockSpec((rows, D_MODEL), lambda l, j: (0, 0)),
                  pl.BlockSpec((1, D_MODEL, tn), lambda l, j: (l, 0, j)),
                  pl.BlockSpec((1, 1, tn), lambda l, j: (l, 0, j))],
        out_specs=pl.BlockSpec((1, rows, tn), lambda l, j: (l, 0, j)),
        out_shape=jax.ShapeDtypeStruct((depth, rows, n), F32),
        compiler_params=_cparams(("arbitrary", "arbitrary")),
        name="ada_modulation",
    )(cond, w_ada, b_ada.reshape(depth, 1, n))


def _ffn_kernel(xp_ref, xm_ref, xn_ref, mod_ref, gain_ref, wup_ref, cw_ref, cb_ref, wdn_ref, gfin_ref,
                out_ref, hb_ref, acc_ref, *, seg, nblk, final):
    t = pl.program_id(1)
    mod = mod_ref[0]
    shift, scale, gate = _mod_part(mod, 3), _mod_part(mod, 4), _mod_part(mod, 5)
    _fill_halo_block(hb_ref, xp_ref, xm_ref, xn_ref, gain_ref[...], scale, shift, t, nblk, seg)
    acc_ref[...] = jnp.zeros_like(acc_ref)

    def tile(c, carry):
        hb = hb_ref[...]
        a = _dwconv3(_dot(hb, wup_ref[0, c]), cw_ref[0, c], cb_ref[0, c], ROW_BLOCK)
        g = _dwconv3(_dot(hb, wup_ref[1, c]), cw_ref[1, c], cb_ref[1, c], ROW_BLOCK)
        acc_ref[...] += _dot((_silu(g) * a).astype(BF16), wdn_ref[c])
        return carry

    lax.fori_loop(0, FFN_DIM // FFN_TILE, tile, 0)
    y = xm_ref[0] + gate * acc_ref[...]
    if final:
        y = _rms(y, gfin_ref[...])
    out_ref[0] = y


def _conv_ffn(x, mods, gain, w_up, conv_w, conv_b, w_down, gain_final, *, seg, final):
    bsz, s, _ = x.shape
    nblk = s // ROW_BLOCK
    nt = FFN_DIM // FFN_TILE
    wup = w_up.astype(BF16).reshape(D_MODEL, 2, nt, FFN_TILE).transpose(1, 2, 0, 3)
    cw = conv_w.reshape(3, 2, nt, FFN_TILE).transpose(1, 2, 0, 3)
    cb = conv_b.reshape(2, nt, 1, FFN_TILE)
    wdn = w_down.astype(BF16).reshape(nt, FFN_TILE, D_MODEL)
    prev, main, nxt = _halo_specs(nblk)
    return pl.pallas_call(
        functools.partial(_ffn_kernel, seg=seg, nblk=nblk, final=final),
        grid=(bsz, nblk),
        in_specs=[prev, main, nxt, _mod_spec(1 - seg), _resident((1, D_MODEL)),
                  _resident(wup.shape), _resident(cw.shape), _resident(cb.shape), _resident(wdn.shape),
                  _resident((1, D_MODEL))],
        out_specs=pl.BlockSpec((1, ROW_BLOCK, D_MODEL), lambda b, t: (b, t, 0)),
        out_shape=jax.ShapeDtypeStruct((bsz, s, D_MODEL), F32),
        scratch_shapes=[pltpu.VMEM((ROW_BLOCK + 2 * HALO, D_MODEL), BF16),
                        pltpu.VMEM((ROW_BLOCK, D_MODEL), F32)],
        compiler_params=_cparams(("arbitrary", "arbitrary")),
        name="conv_ffn",
    )(x, x, x, mods, gain.reshape(1, D_MODEL), wup, cw, cb, wdn, gain_final.reshape(1, D_MODEL))


def _mlstm_proj_kernel(xp_ref, xm_ref, xn_ref, mod_ref, gain_ref, wqk_ref, cw_ref, cb_ref, wvo_ref, wg_ref,
                       bg_ref, q_ref, k_ref, v_ref, o_ref, g_ref, hb_ref, *, nblk):
    t = pl.program_id(1)
    mod = mod_ref[0]
    shift, scale = _mod_part(mod, 0), _mod_part(mod, 1)
    _fill_halo_block(hb_ref, xp_ref, xm_ref, xn_ref, gain_ref[...], scale, shift, t, nblk, 1)
    qk =_silu(_dwconv3(_dot(hb_ref[...], wqk_ref[...]), cw_ref[...], cb_ref[...], ROW_BLOCK))
    q_ref[0] = (qk[:, :A_QK] * (A_DK ** -0.5)).astype(BF16)
    k_ref[0] = qk[:, A_QK:].astype(BF16)
    hm = hb_ref[HALO:HALO + ROW_BLOCK]
    vo = _dot(hm, wvo_ref[...])
    v_ref[0] = vo[:, :A_V].astype(BF16)
    o_ref[0] = vo[:, A_V:]
    g_ref[0] = _dot(hm, wg_ref[...]) + bg_ref[...]


def _mlstm_proj(x, mods, gain, w_in, b_gate, conv_w, conv_b):
    bsz, s, _ = x.shape
    nblk = s // ROW_BLOCK
    wqk = w_in[:, :2 * A_QK].astype(BF16)
    wvo = w_in[:, 2 * A_QK:2 * A_QK + 2 * A_V].astype(BF16)
    ng = 4 * A_HEADS
    wg = jnp.pad(w_in[:, 2 * A_QK + 2 * A_V:], ((0, 0), (0, GATE_PAD - ng))).astype(BF16)
    bg = jnp.pad(b_gate, (0, GATE_PAD - ng)).reshape(1, GATE_PAD)
    prev, main, nxt = _halo_specs(nblk)
    return pl.pallas_call(
        functools.partial(_mlstm_proj_kernel, nblk=nblk),
        grid=(bsz, nblk),
        in_specs=[prev, main, nxt, _mod_spec(0), _resident((1, D_MODEL)),
                  _resident(wqk.shape), _resident((3, 2 * A_QK)), _resident((1, 2 * A_QK)),
                  _resident(wvo.shape), _resident(wg.shape), _resident(bg.shape)],
        out_specs=[_row_spec(A_QK), _row_spec(A_QK), _row_spec(A_V), _row_spec(A_V), _row_spec(GATE_PAD)],
        out_shape=[jax.ShapeDtypeStruct((bsz, s, A_QK), BF16), jax.ShapeDtypeStruct((bsz, s, A_QK), BF16),
                   jax.ShapeDtypeStruct((bsz, s, A_V), BF16), jax.ShapeDtypeStruct((bsz, s, A_V), F32),
                   jax.ShapeDtypeStruct((bsz, s, GATE_PAD), F32)],
        scratch_shapes=[pltpu.VMEM((ROW_BLOCK + 2 * HALO, D_MODEL), BF16)],
        compiler_params=_cparams(("arbitrary", "arbitrary")),
        name="mlstm_proj",
    )(x, x, x, mods, gain.reshape(1, D_MODEL), wqk, conv_w, conv_b.reshape(1, -1), wvo, wg, bg)


def _scan_chunk_index(step, n_chunks, n_ctx_chunks, reverse):
    if not reverse:
        return step
    return jnp.where(step < n_ctx_chunks, n_ctx_chunks - 1 - step, n_chunks - 1 + n_ctx_chunks - step)


def _mlstm_scan_kernel(q_ref, k_ref, v_ref, gc_ref, gr_ref, h_ref, state_ref, m_ref, *, reverse):
    L = A_CHUNK
    step = pl.program_id(1)

    @pl.when(step == 0)
    def _():
        state_ref[...] = jnp.zeros_like(state_ref)
        m_ref[...] = jnp.zeros_like(m_ref)

    row = lax.broadcasted_iota(jnp.int32, (L, L), 0)
    col = lax.broadcasted_iota(jnp.int32, (L, L), 1)
    keep = (col >= row) if reverse else (col <= row)
    keep_bf16 = jnp.where(keep, 1.0, 0.0).astype(BF16)
    i_lane = 2 * A_HEADS if reverse else 0
    f_lane = i_lane + A_HEADS

    gc = gc_ref[0]
    gr = gr_ref[0]
    logf_c = _log_sigmoid(gc)
    logf_r = _log_sigmoid(gr)
    cum_c = _cumsum_rows(keep_bf16, logf_c)
    cum_r = _cumsum_cols(logf_r, keep_bf16)
    total = jnp.sum(logf_c, axis=0, keepdims=True)

    q = q_ref[0]
    k = k_ref[0]
    v = v_ref[0]
    ones = jnp.ones((L, A_DV), BF16)
    for h in range(A_HEADS):
        qh = q[:, h * A_DK:(h + 1) * A_DK]
        kh = k[:, h * A_DK:(h + 1) * A_DK]
        vaug = jnp.concatenate([v[:, h * A_DV:(h + 1) * A_DV], ones], axis=1)
        g_c = cum_c[:, f_lane + h:f_lane + h + 1]
        i_c = gc[:, i_lane + h:i_lane + h + 1]
        g_r = cum_r[f_lane + h:f_lane + h + 1, :]
        i_r = gr[i_lane + h:i_lane + h + 1, :]
        g_tot = total[:, f_lane + h:f_lane + h + 1]
        m_prev = m_ref[h:h + 1, 0:1]

        log_d = jnp.where(keep, g_c - g_r + i_r, -jnp.inf)
        m_inter = g_c + m_prev
        m_row = jnp.maximum(jnp.max(log_d, axis=-1, keepdims=True), m_inter)
        p = (_dot_nt(qh, kh) * jnp.exp(log_d - m_row)).astype(BF16)
        inter = jnp.exp(m_inter - m_row)
        state = state_ref[h]
        nd = _dot(p, vaug) + inter * _dot(qh, state.astype(BF16))
        den = jnp.maximum(jnp.abs(nd[:, A_DV:A_DV + 1]), jnp.exp(-m_row))
        h_ref[0, :, h * A_DV:(h + 1) * A_DV] = nd[:, :A_DV] / den

        log_w = g_tot - g_c + i_c
        m_new = jnp.maximum(g_tot + m_prev, jnp.max(log_w, axis=0, keepdims=True))
        kw = (kh.astype(F32) * jnp.exp(log_w - m_new)).astype(BF16)
        state_ref[h] = jnp.exp(g_tot + m_prev - m_new) * state + _dot_tn(kw, vaug)
        m_ref[h:h + 1, :] = jnp.broadcast_to(m_new, (1, m_ref.shape[1]))


def _mlstm_scan(q, k, v, gates_c, gates_r, *, reverse):
    bsz, s, _ = q.shape
    L = A_CHUNK
    n_chunks, n_ctx = s // L, CTX_LEN // L
    cidx = functools.partial(_scan_chunk_index, n_chunks=n_chunks, n_ctx_chunks=n_ctx, reverse=reverse)
    return pl.pallas_call(
        functools.partial(_mlstm_scan_kernel, reverse=reverse),
        grid=(bsz, n_chunks),
        in_specs=[pl.BlockSpec((1, L, A_QK), lambda b, c: (b, cidx(c), 0)),
                  pl.BlockSpec((1, L, A_QK), lambda b, c: (b, cidx(c), 0)),
                  pl.BlockSpec((1, L, A_V), lambda b, c: (b, cidx(c), 0)),
                  pl.BlockSpec((1, L, GATE_PAD), lambda b, c: (b, cidx(c), 0)),
                  pl.BlockSpec((1, 4 * A_HEADS, L), lambda b, c: (b, 0, cidx(c)))],
        out_specs=pl.BlockSpec((1, L, A_V), lambda b, c: (b, cidx(c), 0)),
        out_shape=jax.ShapeDtypeStruct((bsz, s, A_V), F32),
        scratch_shapes=[pltpu.VMEM((A_HEADS, A_DK, 2 * A_DV), F32), pltpu.VMEM((A_HEADS, 128), F32)],
        compiler_params=_cparams(("arbitrary", "arbitrary")),
        name="mlstm_scan_bwd" if reverse else "mlstm_scan_fwd",
    )(q, k, v, gates_c, gates_r)


def _mlstm_out_kernel(x_ref, hf_ref, hb_ref, o_ref, mod_ref, w_ref, out_ref):
    z = ((hf_ref[0] + hb_ref[0]) * jax.nn.sigmoid(o_ref[0])).astype(BF16)
    out_ref[0] = x_ref[0] + _mod_part(mod_ref[0], 2) * _dot(z, w_ref[...])


def _gla_out_kernel(x_ref, of_ref, ob_ref, r_ref, gn_ref, mod_ref, w_ref, out_ref):
    oo = of_ref[0] + ob_ref[0]
    parts = [_rms(oo[:, h * B_DV:(h + 1) * B_DV], gn_ref[...]) for h in range(B_HEADS)]
    z = (jnp.concatenate(parts, axis=1) * _silu(r_ref[0])).astype(BF16)
    out_ref[0] = x_ref[0] + _mod_part(mod_ref[0], 2) * _dot(z, w_ref[...])


def _na_out_kernel(x_ref, o_ref, mod_ref, w_ref, out_ref):
    out_ref[0] = x_ref[0] + _mod_part(mod_ref[0], 2) * _dot(o_ref[0], w_ref[...])


def _mixer_out(kernel_fn, name, x, acts, consts, mods, w_out, *, t0):
    bsz, s, _ = x.shape
    nblk = s // ROW_BLOCK
    w = w_out.astype(BF16)
    return pl.pallas_call(
        kernel_fn,
        grid=(bsz, nblk - t0),
        in_specs=([_row_spec(D_MODEL, t0)] + [_row_spec(a.shape[-1], t0) for a in acts]
                  + [_resident(c.shape) for c in consts] + [_mod_spec(t0), _resident(w.shape)]),
        out_specs=_row_spec(D_MODEL),
        out_shape=jax.ShapeDtypeStruct((bsz, s - t0 * ROW_BLOCK, D_MODEL), F32),
        compiler_params=_cparams(("arbitrary", "arbitrary")),
        name=name,
    )(x, *acts, *consts, mods, w)


def _gla_proj_kernel(x_ref, mod_ref, gain_ref, wqkv_ref, wr_ref, wa_ref, q_ref, k_ref, v_ref, r_ref, a_ref):
    mod = mod_ref[0]
    h = _norm_mod(x_ref[0], gain_ref[...], _mod_part(mod, 1), _mod_part(mod, 0)).astype(BF16)
    qkv = _dot(h, wqkv_ref[...])
    q_ref[0] = qkv[:, :B_QK].astype(BF16)
    k_ref[0] = qkv[:, B_QK:2 * B_QK].astype(BF16)
    v_ref[0] = qkv[:, 2 * B_QK:].astype(BF16)
    r_ref[0] = _dot(h, wr_ref[...])
    a_ref[0] = _dot(h, wa_ref[...])


def _gla_proj(x, mods, gain, w_in):
    bsz, s, _ = x.shape
    nblk = s // ROW_BLOCK
    wqkv = w_in[:, :2 * B_QK + B_V].astype(BF16)
    wr = w_in[:, 2 * B_QK + B_V:2 * B_QK + 2 * B_V].astype(BF16)
    wa = jnp.pad(w_in[:, 2 * B_QK + 2 * B_V:], ((0, 0), (0, GATE_PAD - 2 * B_RANK))).astype(BF16)
    return pl.pallas_call(
        _gla_proj_kernel,
        grid=(bsz, nblk),
        in_specs=[_row_spec(D_MODEL), _mod_spec(0), _resident((1, D_MODEL)),
                  _resident(wqkv.shape), _resident(wr.shape), _resident(wa.shape)],
        out_specs=[_row_spec(B_QK), _row_spec(B_QK), _row_spec(B_V), _row_spec(B_V), _row_spec(GATE_PAD)],
        out_shape=[jax.ShapeDtypeStruct((bsz, s, B_QK), BF16), jax.ShapeDtypeStruct((bsz, s, B_QK), BF16),
                   jax.ShapeDtypeStruct((bsz, s, B_V), BF16), jax.ShapeDtypeStruct((bsz, s, B_V), F32),
                   jax.ShapeDtypeStruct((bsz, s, GATE_PAD), F32)],
        compiler_params=_cparams(("arbitrary", "arbitrary")),
        name="gla_proj",
    )(x, mods, gain.reshape(1, D_MODEL), wqkv, wr, wa)


def _gla_scan_kernel(q_ref, k_ref, v_ref, a_ref, wa_ref, ba_ref, o_ref, state_ref, *, reverse):
    L, sub = B_CHUNK, B_SUB
    nsub = L // sub
    step = pl.program_id(1)

    @pl.when(step == 0)
    def _():
        state_ref[...] = jnp.zeros_like(state_ref)

    row = lax.broadcasted_iota(jnp.int32, (L, L), 0)
    col = lax.broadcasted_iota(jnp.int32, (L, L), 1)
    keep = (col >= row) if reverse else (col <= row)
    keep_bf16 = jnp.where(keep, 1.0, 0.0).astype(BF16)
    cross = (col // sub > row // sub) if reverse else (col // sub < row // sub)
    row_sub = lax.broadcasted_iota(jnp.int32, (L, B_DK), 0) // sub

    pair = lax.broadcasted_iota(jnp.int32, (L * sub, 128), 0)
    lane = lax.broadcasted_iota(jnp.int32, (L * sub, 128), 1)
    pair_q, pair_s = pair // sub, pair % sub
    place = lane == (pair_q // sub) * sub + pair_s
    sel = (lax.broadcasted_iota(jnp.int32, (L, L * sub), 1) // sub
           == lax.broadcasted_iota(jnp.int32, (L, L * sub), 0))
    sel_bf16 = jnp.where(sel, 1.0, 0.0).astype(BF16)
    ones_red = jnp.ones((B_DK, 128), BF16)
    srow = lax.broadcasted_iota(jnp.int32, (sub, B_DK), 0)

    z = _dot(a_ref[0].astype(BF16), wa_ref[...]) + ba_ref[...]
    g = _log_sigmoid(z) / B_TAU
    cum = _cumsum_rows(keep_bf16, g)

    q = q_ref[0]
    k = k_ref[0]
    v = v_ref[0]
    for h in range(B_HEADS):
        G = cum[:, h * B_DK:(h + 1) * B_DK]
        qh = q[:, h * B_DK:(h + 1) * B_DK].astype(F32) * (B_DK ** -0.5)
        kh = k[:, h * B_DK:(h + 1) * B_DK].astype(F32)
        vh = v[:, h * B_DV:(h + 1) * B_DV]
        g_tot = G[0:1] if reverse else G[L - 1:L]

        state_t = state_ref[h]
        o = _dot_nt((qh * jnp.exp(G)).astype(BF16), state_t.astype(BF16))

        def edge(b):
            r = b * sub if reverse else b * sub + sub - 1
            return G[r:r + 1]
        edges = jnp.concatenate([jnp.broadcast_to(edge(b), (sub, B_DK)) for b in range(nsub)], axis=0)
        k_edge = kh * jnp.exp(edges - G)
        q_cat = jnp.concatenate(
            [(qh * jnp.exp(jnp.minimum(G - edge(b), 0.0))).astype(BF16) for b in range(nsub)], axis=1)
        k_cat = jnp.concatenate(
            [jnp.where(row_sub == b, k_edge, 0.0).astype(BF16) for b in range(nsub)], axis=1)
        att = jnp.where(cross, _dot_nt(q_cat, k_cat), 0.0)

        pieces = []
        for b in range(nsub):
            Gb = G[b * sub:(b + 1) * sub]
            kb = kh[b * sub:(b + 1) * sub]
            for j in range(sub):
                r = b * sub + j
                vis = (srow >= j) if reverse else (srow <= j)
                e = jnp.exp(jnp.minimum(G[r:r + 1] - Gb, 0.0))
                pieces.append(jnp.where(vis, qh[r:r + 1] * kb * e, 0.0))
        pair_terms = jnp.concatenate(pieces, axis=0).astype(BF16)
        pair_sum = _dot(pair_terms, ones_red)
        local = _dot(sel_bf16, jnp.where(place, pair_sum, 0.0).astype(BF16))
        att = att + local[:, :L]

        o_ref[0, :, h * B_DV:(h + 1) * B_DV] = o + _dot(att.astype(BF16), vh)
        k_tot = (kh * jnp.exp(g_tot - G)).astype(BF16)
        state_ref[h] = state_t * jnp.exp(g_tot) + _dot_tn(vh, k_tot)


def _gla_scan(q, k, v, a, w_a2, b_a, *, reverse):
    bsz, s, _ = q.shape
    L = B_CHUNK
    n_chunks, n_ctx = s // L, CTX_LEN // L
    d = 1 if reverse else 0
    wa = jnp.zeros((GATE_PAD, B_QK), F32).at[d * B_RANK:(d + 1) * B_RANK].set(w_a2[d]).astype(BF16)
    ba = b_a[d].reshape(1, B_QK)
    cidx = functools.partial(_scan_chunk_index, n_chunks=n_chunks, n_ctx_chunks=n_ctx, reverse=reverse)
    return pl.pallas_call(
        functools.partial(_gla_scan_kernel, reverse=reverse),
        grid=(bsz, n_chunks),
        in_specs=[pl.BlockSpec((1, L, B_QK), lambda b, c: (b, cidx(c), 0)),
                  pl.BlockSpec((1, L, B_QK), lambda b, c: (b, cidx(c), 0)),
                  pl.BlockSpec((1, L, B_V), lambda b, c: (b, cidx(c), 0)),
                  pl.BlockSpec((1, L, GATE_PAD), lambda b, c: (b, cidx(c), 0)),
                  _resident(wa.shape), _resident(ba.shape)],
        out_specs=pl.BlockSpec((1, L, B_V), lambda b, c: (b, cidx(c), 0)),
        out_shape=jax.ShapeDtypeStruct((bsz, s, B_V), F32),
        scratch_shapes=[pltpu.VMEM((B_HEADS, B_DV, B_DK), F32)],
        compiler_params=_cparams(("arbitrary", "arbitrary")),
        name="gla_scan_bwd" if reverse else "gla_scan_fwd",
    )(q, k, v, a, wa, ba)


def _na_proj_kernel(x_ref, mod_ref, gain_ref, w_ref, q_ref, k_ref, v_ref):
    mod = mod_ref[0]
    h = _norm_mod(x_ref[0], gain_ref[...], _mod_part(mod, 1), _mod_part(mod, 0)).astype(BF16)
    qkv = _dot(h, w_ref[...])
    q_ref[0] = (qkv[:, :D_MODEL] * (C_DH ** -0.5)).astype(BF16)
    k_ref[0] = qkv[:, D_MODEL:2 * D_MODEL].astype(BF16)
    v_ref[0] = qkv[:, 2 * D_MODEL:].astype(BF16)


def _na_proj(x, mods, gain, w_in):
    bsz, s, _ = x.shape
    w = w_in.astype(BF16)
    return pl.pallas_call(
        _na_proj_kernel,
        grid=(bsz, s // ROW_BLOCK),
        in_specs=[_row_spec(D_MODEL), _mod_spec(0), _resident((1, D_MODEL)), _resident(w.shape)],
        out_specs=[_row_spec(D_MODEL)] * 3,
        out_shape=[jax.ShapeDtypeStruct((bsz, s, D_MODEL), BF16)] * 3,
        compiler_params=_cparams(("arbitrary", "arbitrary")),
        name="na_proj",
    )(x, mods, gain.reshape(1, D_MODEL), w)


def _na_bias_table(rpb):
    colq = np.arange(GRID_W)[:, None]
    colk = np.arange(GRID_W)[None, :]
    start = np.clip(colq - C_WIN_COLS // 2, 0, GRID_W - C_WIN_COLS)
    valid = (colk >= start) & (colk < start + C_WIN_COLS)
    dc = np.clip(colk - colq + C_WIN_COLS - 1, 0, 2 * C_WIN_COLS - 2)
    bias = jnp.where(valid[None, None], rpb[:, :, dc], -jnp.inf)
    return jnp.concatenate([bias[:, :-1], bias[:, 1:]], axis=-1)


def _na_kernel(q_ref, k_ref, v_ref, tab_ref, o_ref, *, rows):
    step = pl.program_id(1)
    n_ctx_blk = CTX_LEN // GRID_W
    kc = k_ref[0, 0:CTX_LEN, :]
    vc = v_ref[0, 0:CTX_LEN, :]
    q = q_ref[0]

    def finish(h, p_parts, v_parts, denom):
        acc = _dot(p_parts[0].astype(BF16), v_parts[0])
        for p, vv in zip(p_parts[1:], v_parts[1:]):
            acc = acc + _dot(p.astype(BF16), vv)
        o_ref[0, :, h * C_DH:(h + 1) * C_DH] = (acc / denom).astype(BF16)

    @pl.when(step < n_ctx_blk)
    def _():
        for h in range(C_HEADS):
            hs = slice(h * C_DH, (h + 1) * C_DH)
            s_ctx = _dot_nt(q[:, hs], kc[:, hs])
            p_ctx = jnp.exp(s_ctx - jnp.max(s_ctx, axis=-1, keepdims=True))
            finish(h, [p_ctx], [vc[:, hs]], jnp.sum(p_ctx, axis=-1, keepdims=True))

    @pl.when(step >= n_ctx_blk)
    def _():
        r = step - n_ctx_blk
        rs = jnp.clip(r - C_WIN_ROWS // 2, 0, rows - C_WIN_ROWS)
        start = pl.multiple_of(CTX_LEN + rs * GRID_W, GRID_W)
        kl = k_ref[0, pl.ds(start, C_WIN_ROWS * GRID_W), :]
        vl = v_ref[0, pl.ds(start, C_WIN_ROWS * GRID_W), :]
        e0 = rs - r + C_WIN_ROWS - 1
        for h in range(C_HEADS):
            hs = slice(h * C_DH, (h + 1) * C_DH)
            bias = jnp.concatenate([tab_ref[h, e0 + 2 * i] for i in range(C_WIN_ROWS // 2)], axis=1)
            s_loc = _dot_nt(q[:, hs], kl[:, hs]) + bias
            s_ctx = _dot_nt(q[:, hs], kc[:, hs])
            m = jnp.maximum(jnp.max(s_loc, axis=-1, keepdims=True), jnp.max(s_ctx, axis=-1, keepdims=True))
            p_loc = jnp.exp(s_loc - m)
            p_ctx = jnp.exp(s_ctx - m)
            denom = jnp.sum(p_loc, axis=-1, keepdims=True) + jnp.sum(p_ctx, axis=-1, keepdims=True)
            finish(h, [p_loc, p_ctx], [vl[:, hs], vc[:, hs]], denom)


def _na_attention(q, k, v, rpb):
    bsz, s, _ = q.shape
    rows = (s - CTX_LEN) // GRID_W
    assert rows >= C_WIN_ROWS
    tab = _na_bias_table(rpb)
    return pl.pallas_call(
        functools.partial(_na_kernel, rows=rows),
        grid=(bsz, s // GRID_W),
        in_specs=[pl.BlockSpec((1, GRID_W, D_MODEL), lambda b, j: (b, j, 0)),
                  pl.BlockSpec((1, s, D_MODEL), lambda b, j: (b, 0, 0)),
                  pl.BlockSpec((1, s, D_MODEL), lambda b, j: (b, 0, 0)),
                  _resident(tab.shape)],
        out_specs=pl.BlockSpec((1, GRID_W, D_MODEL), lambda b, j: (b, j, 0)),
        out_shape=jax.ShapeDtypeStruct((bsz, s, D_MODEL), BF16),
        compiler_params=_cparams(("arbitrary", "arbitrary")),
        name="na_attention",
    )(q, k, v, tab)


def kernel(x, c, ctx, c_ctx, w_ada, b_ada, norm_mix, norm_ffn, w_up, ffn_conv_w, ffn_conv_b, w_down,
           a_w_in, a_b_gate, a_conv_w, a_conv_b, a_w_out,
           b_w_in, b_w_a2, b_b_a, b_norm, b_w_out,
           c_w_in, c_rpb, c_w_out, norm_final):
    bsz = x.shape[0]
    depth = w_ada.shape[0]
    assert ctx.shape[1] == CTX_LEN == ROW_BLOCK

    cond = jnp.concatenate([c, c_ctx[None], jnp.zeros((16 - bsz - 1, D_MODEL), F32)], axis=0)
    mod_all = _ada_modulation(cond, w_ada, b_ada)
    stream = jnp.concatenate([ctx, x], axis=1)

    for i in range(depth):
        last = i == depth - 1
        t0 = 1 if last else 0
        m = mod_all[i]
        mods = jnp.stack([jnp.broadcast_to(m[bsz], (bsz, 6 * D_MODEL)), m[:bsz]], axis=1)
        mods = mods.reshape(2 * bsz, 1, 6 * D_MODEL)
        kind, j = i % 3, i // 3
        if kind == 0:
            q, k, v, o, gates = _mlstm_proj(stream, mods, norm_mix[i], a_w_in[j], a_b_gate[j],
                                            a_conv_w[j], a_conv_b[j])
            gates_r = jnp.swapaxes(gates[:, :, :4 * A_HEADS], 1, 2)
            hf = _mlstm_scan(q, k, v, gates, gates_r, reverse=False)
            hb = _mlstm_scan(q, k, v, gates, gates_r, reverse=True)
            stream_mid = _mixer_out(_mlstm_out_kernel, "mlstm_out", stream, [hf, hb, o], [], mods,
                                    a_w_out[j], t0=t0)
        elif kind == 1:
            q, k, v, r, a = _gla_proj(stream, mods, norm_mix[i], b_w_in[j])
            of = _gla_scan(q, k, v, a, b_w_a2[j], b_b_a[j], reverse=False)
            ob = _gla_scan(q, k, v, a, b_w_a2[j], b_b_a[j], reverse=True)
            stream_mid = _mixer_out(_gla_out_kernel, "gla_out", stream, [of, ob, r],
                                    [b_norm[j].reshape(1, B_DV)], mods, b_w_out[j], t0=t0)
        else:
            q, k, v = _na_proj(stream, mods, norm_mix[i], c_w_in[j])
            o = _na_attention(q, k, v, c_rpb[j])
            stream_mid = _mixer_out(_na_out_kernel, "na_out", stream, [o], [], mods, c_w_out[j], t0=t0)
        stream = _conv_ffn(stream_mid, mods, norm_ffn[i], w_up[i], ffn_conv_w[i], ffn_conv_b[i], w_down[i],
                           norm_final, seg=1 - t0, final=last)
    return stream
```

```python
import functools

import jax
import jax.numpy as jnp
import numpy as np
from jax import lax
from jax.experimental import pallas as pl
from jax.experimental.pallas import tpu as pltpu

F32 = jnp.float32
BF16 = jnp.bfloat16

D_MODEL = 1024
EPS = 1e-6
GRID_W = 64

HALO = 16
VMEM_LIMIT = 56 * 1024 * 1024
COND_ROWS = 16

A_HEADS, A_DK, A_DV, A_CHUNK = 8, 64, 128, 128
A_QK, A_V = A_HEADS * A_DK, A_HEADS * A_DV
B_HEADS, B_DK, B_DV, B_CHUNK = 4, 128, 256, 64
B_QK, B_V, B_RANK, B_TAU = B_HEADS * B_DK, B_HEADS * B_DV, 16, 16.0
B_SUB = 8
C_HEADS, C_DH, C_WIN_ROWS, C_WIN_COLS = 16, 64, 8, 16
C_GROUP = 4
C_KROWS = C_WIN_ROWS + 2
FFN_DIM = 2816
FFN_TILE = 256
GATE_PAD = 128

NT_DIMS = (((1,), (1,)), ((), ()))
TN_DIMS = (((0,), (0,)), ((), ()))


def _cparams(n_axes=2):
    return pltpu.CompilerParams(dimension_semantics=("arbitrary",) * n_axes, vmem_limit_bytes=VMEM_LIMIT)


def _resident(shape):
    nd = len(shape)
    return pl.BlockSpec(shape, lambda *_: (0,) * nd, pipeline_mode=pl.Buffered(1))


def _block_rows(n_tokens):
    return 512 if n_tokens % 512 == 0 else 256


def _dot(a, b):
    return jnp.dot(a, b, preferred_element_type=F32)


def _dot_nt(a, b):
    return lax.dot_general(a, b, NT_DIMS, preferred_element_type=F32)


def _dot_tn(a, b):
    return lax.dot_general(a, b, TN_DIMS, preferred_element_type=F32)


def _split3(x):
    x1 = x.astype(BF16)
    r1 = x - x1.astype(F32)
    x2 = r1.astype(BF16)
    x3 = (r1 - x2.astype(F32)).astype(BF16)
    return x1, x2, x3


def _cumsum_rows(mask_bf16, x):
    x1, x2, x3 = _split3(x)
    return _dot(mask_bf16, x1) + _dot(mask_bf16, x2) + _dot(mask_bf16, x3)


def _cumsum_cols(x, mask_bf16):
    x1, x2, x3 = _split3(x)
    return _dot_nt(x1, mask_bf16) + _dot_nt(x2, mask_bf16) + _dot_nt(x3, mask_bf16)


def _log_sigmoid(z):
    return jnp.minimum(z, 0.0) - jnp.log1p(jnp.exp(-jnp.abs(z)))


def _silu(z):
    return z * jax.nn.sigmoid(z)


def _rms(x, gain):
    return x * lax.rsqrt(jnp.mean(x * x, axis=-1, keepdims=True) + EPS) * gain


def _norm_mod(x, gain, scale, shift):
    return _rms(x, gain) * (1.0 + scale) + shift


def _mod_part(mod, i):
    return mod[:, i * D_MODEL:(i + 1) * D_MODEL]


def _dwconv3(u, w, b, rows):
    n = rows + 2 * HALO
    prev = pltpu.roll(u, 1, 0)[HALO:HALO + rows]
    nxt = pltpu.roll(u, n - 1, 0)[HALO:HALO + rows]
    return prev * w[0:1] + u[HALO:HALO + rows] * w[1:2] + nxt * w[2:3] + b


def _fill_halo_block(hb_ref, xp_ref, xm_ref, xn_ref, gain, scale, shift, rows):
    t = pl.program_id(1)
    has_prev = (t != 0).astype(F32)
    has_next = (t != pl.num_programs(1) - 1).astype(F32)
    hb_ref[0:HALO] = (_norm_mod(xp_ref[0], gain, scale, shift) * has_prev).astype(BF16)
    hb_ref[HALO:HALO + rows] = _norm_mod(xm_ref[0], gain, scale, shift).astype(BF16)
    hb_ref[HALO + rows:] = (_norm_mod(xn_ref[0], gain, scale, shift) * has_next).astype(BF16)


def _halo_specs(n_tokens, rows):
    per = rows // HALO
    last = n_tokens // HALO - 1
    prev = pl.BlockSpec((1, HALO, D_MODEL), lambda b, t: (b, jnp.maximum(t * per - 1, 0), 0))
    main = pl.BlockSpec((1, rows, D_MODEL), lambda b, t: (b, t, 0))
    nxt = pl.BlockSpec((1, HALO, D_MODEL), lambda b, t: (b, jnp.minimum((t + 1) * per, last), 0))
    return prev, main, nxt


def _mod_spec(cond_row):
    if cond_row is None:
        return pl.BlockSpec((1, 1, 6 * D_MODEL), lambda b, t: (b, 0, 0))
    return pl.BlockSpec((1, 1, 6 * D_MODEL), lambda b, t: (cond_row, 0, 0))


def _row_spec(rows, width):
    return pl.BlockSpec((1, rows, width), lambda b, t: (b, t, 0))


def _ada_kernel(c_ref, w_ref, b_ref, o_ref):
    s = _silu(c_ref[...]).astype(BF16)
    o_ref[0] = _dot(s, w_ref[0].astype(BF16)) + b_ref[0]


def _ada_modulation(cond, w_ada, b_ada):
    depth, _, n = w_ada.shape
    tn = 1024
    return pl.pallas_call(
        _ada_kernel,
        grid=(depth, n // tn),
        in_specs=[pl.BlockSpec((COND_ROWS, D_MODEL), lambda l, j: (0, 0)),
                  pl.BlockSpec((1, D_MODEL, tn), lambda l, j: (l, 0, j)),
                  pl.BlockSpec((1, 1, tn), lambda l, j: (l, 0, j))],
        out_specs=pl.BlockSpec((1, COND_ROWS, tn), lambda l, j: (l, 0, j)),
        out_shape=jax.ShapeDtypeStruct((depth, COND_ROWS, n), F32),
        compiler_params=_cparams(),
        name="ada_modulation",
    )(cond, w_ada, b_ada.reshape(depth, 1, n))


def _ffn_kernel(xp_ref, xm_ref, xn_ref, mod_ref, gain_ref, wup_ref, cw_ref, cb_ref, wdn_ref, gfin_ref,
                out_ref, hb_ref, z_ref, *, rows, final):
    mod = mod_ref[0]
    shift, scale, gate = _mod_part(mod, 3), _mod_part(mod, 4), _mod_part(mod, 5)
    _fill_halo_block(hb_ref, xp_ref, xm_ref, xn_ref, gain_ref[...], scale, shift, rows)
    for c in range(FFN_DIM // FFN_TILE):
        a = _dwconv3(_dot(hb_ref[...], wup_ref[0, c]), cw_ref[0, c], cb_ref[0, c], rows)
        g = _dwconv3(_dot(hb_ref[...], wup_ref[1, c]), cw_ref[1, c], cb_ref[1, c], rows)
        z_ref[:, c * FFN_TILE:(c + 1) * FFN_TILE] = (_silu(g) * a).astype(BF16)
    y = xm_ref[0] + gate * _dot(z_ref[...], wdn_ref[...])
    if final:
        y = _rms(y, gfin_ref[...])
    out_ref[0] = y


def _ffn_weights(w_up, conv_w, conv_b, w_down):
    nt = FFN_DIM // FFN_TILE
    wup = w_up.astype(BF16).reshape(D_MODEL, 2, nt, FFN_TILE).transpose(1, 2, 0, 3)
    cw = conv_w.reshape(3, 2, nt, FFN_TILE).transpose(1, 2, 0, 3)
    cb = conv_b.reshape(2, nt, 1, FFN_TILE)
    return wup, cw, cb, w_down.astype(BF16)


def _conv_ffn(x, mods, cond_row, gain, weights, gain_final, *, final):
    bsz, n, _ = x.shape
    rows = _block_rows(n)
    wup, cw, cb, wdn = weights
    prev, main, nxt = _halo_specs(n, rows)
    return pl.pallas_call(
        functools.partial(_ffn_kernel, rows=rows, final=final),
        grid=(bsz, n // rows),
        in_specs=[prev, main, nxt, _mod_spec(cond_row), _resident((1, D_MODEL)),
                  _resident(wup.shape), _resident(cw.shape), _resident(cb.shape), _resident(wdn.shape),
                  _resident((1, D_MODEL))],
        out_specs=_row_spec(rows, D_MODEL),
        out_shape=jax.ShapeDtypeStruct((bsz, n, D_MODEL), F32),
        scratch_shapes=[pltpu.VMEM((rows + 2 * HALO, D_MODEL), BF16),
                        pltpu.VMEM((rows, FFN_DIM), BF16)],
        compiler_params=_cparams(),
        name="conv_ffn",
    )(x, x, x, mods, gain.reshape(1, D_MODEL), wup, cw, cb, wdn, gain_final.reshape(1, D_MODEL))


def _mlstm_out_kernel(x_ref, hf_ref, hb_ref, o_ref, mod_ref, w_ref, out_ref):
    z = ((hf_ref[0] + hb_ref[0]) * jax.nn.sigmoid(o_ref[0])).astype(BF16)
    out_ref[0] = x_ref[0] + _mod_part(mod_ref[0], 2) * _dot(z, w_ref[...])


def _gla_out_kernel(x_ref, of_ref, ob_ref, r_ref, gn_ref, mod_ref, w_ref, out_ref):
    oo = of_ref[0] + ob_ref[0]
    parts = [_rms(oo[:, h * B_DV:(h + 1) * B_DV], gn_ref[...]) for h in range(B_HEADS)]
    z = (jnp.concatenate(parts, axis=1) * _silu(r_ref[0])).astype(BF16)
    out_ref[0] = x_ref[0] + _mod_part(mod_ref[0], 2) * _dot(z, w_ref[...])


def _na_out_kernel(x_ref, o_ref, mod_ref, w_ref, out_ref):
    out_ref[0] = x_ref[0] + _mod_part(mod_ref[0], 2) * _dot(o_ref[0], w_ref[...])


def _na_out_t_kernel(x_ref, ot_ref, mod_ref, w_ref, out_ref):
    out_ref[0] = x_ref[0] + _mod_part(mod_ref[0], 2) * _dot_tn(ot_ref[0], w_ref[...])


def _mixer_out(kernel_fn, name, x, act_specs, acts, consts, mods, cond_row, w_out):
    bsz, n, _ = x.shape
    rows = _block_rows(n)
    w = w_out.astype(BF16)
    specs = [s if s is not None else _row_spec(rows, a.shape[-1]) for s, a in zip(act_specs, acts)]
    return pl.pallas_call(
        kernel_fn,
        grid=(bsz, n // rows),
        in_specs=([_row_spec(rows, D_MODEL)] + specs + [_resident(c.shape) for c in consts]
                  + [_mod_spec(cond_row), _resident(w.shape)]),
        out_specs=_row_spec(rows, D_MODEL),
        out_shape=jax.ShapeDtypeStruct((bsz, n, D_MODEL), F32),
        compiler_params=_cparams(),
        name=name,
    )(x, *acts, *consts, mods, w)


def _mlstm_proj_kernel(xp_ref, xm_ref, xn_ref, mod_ref, gain_ref, wqk_ref, cw_ref, cb_ref, wvo_ref, wg_ref,
                       bg_ref, q_ref, k_ref, v_ref, o_ref, g_ref, hb_ref, *, rows):
    mod = mod_ref[0]
    shift, scale = _mod_part(mod, 0), _mod_part(mod, 1)
    _fill_halo_block(hb_ref, xp_ref, xm_ref, xn_ref, gain_ref[...], scale, shift, rows)
    qk = _silu(_dwconv3(_dot(hb_ref[...], wqk_ref[...]), cw_ref[...], cb_ref[...], rows))
    q_ref[0] = (qk[:, :A_QK] * (A_DK ** -0.5)).astype(BF16)
    k_ref[0] = qk[:, A_QK:].astype(BF16)
    hm = hb_ref[HALO:HALO + rows]
    vo = _dot(hm, wvo_ref[...])
    v_ref[0] = vo[:, :A_V].astype(BF16)
    o_ref[0] = vo[:, A_V:]
    g_ref[0] = _dot(hm, wg_ref[...]) + bg_ref[...]


def _mlstm_weights(w_in, b_gate, conv_w, conv_b):
    ng = 4 * A_HEADS
    wqk = w_in[:, :2 * A_QK].astype(BF16)
    wvo = w_in[:, 2 * A_QK:2 * A_QK + 2 * A_V].astype(BF16)
    wg = jnp.pad(w_in[:, 2 * A_QK + 2 * A_V:], ((0, 0), (0, GATE_PAD - ng))).astype(BF16)
    bg = jnp.pad(b_gate, (0, GATE_PAD - ng)).reshape(1, GATE_PAD)
    return wqk, conv_w, conv_b.reshape(1, -1), wvo, wg, bg


def _mlstm_proj(x, mods, cond_row, gain, weights):
    bsz, n, _ = x.shape
    rows = _block_rows(n)
    prev, main, nxt = _halo_specs(n, rows)
    return pl.pallas_call(
        functools.partial(_mlstm_proj_kernel, rows=rows),
        grid=(bsz, n // rows),
        in_specs=[prev, main, nxt, _mod_spec(cond_row), _resident((1, D_MODEL))]
                 + [_resident(w.shape) for w in weights],
        out_specs=[_row_spec(rows, A_QK), _row_spec(rows, A_QK), _row_spec(rows, A_V), _row_spec(rows, A_V),
                   _row_spec(rows, GATE_PAD)],
        out_shape=[jax.ShapeDtypeStruct((bsz, n, A_QK), BF16), jax.ShapeDtypeStruct((bsz, n, A_QK), BF16),
                   jax.ShapeDtypeStruct((bsz, n, A_V), BF16), jax.ShapeDtypeStruct((bsz, n, A_V), F32),
                   jax.ShapeDtypeStruct((bsz, n, GATE_PAD), F32)],
        scratch_shapes=[pltpu.VMEM((rows + 2 * HALO, D_MODEL), BF16)],
        compiler_params=_cparams(),
        name="mlstm_proj",
    )(x, x, x, mods, gain.reshape(1, D_MODEL), *weights)


def _mlstm_scan_kernel(q_ref, k_ref, v_ref, gc_ref, gr_ref, s0_ref, m0_ref, h_ref, s_out_ref, m_out_ref,
                       state_ref, m_ref, *, reverse):
    L = A_CHUNK
    step = pl.program_id(1)

    @pl.when(step == 0)
    def _():
        state_ref[...] = s0_ref[0]
        m_ref[...] = m0_ref[0]

    row = lax.broadcasted_iota(jnp.int32, (L, L), 0)
    col = lax.broadcasted_iota(jnp.int32, (L, L), 1)
    keep = (col >= row) if reverse else (col <= row)
    keep_bf16 = jnp.where(keep, 1.0, 0.0).astype(BF16)
    i_lane = 2 * A_HEADS if reverse else 0
    f_lane = i_lane + A_HEADS

    gc = gc_ref[0]
    gr = gr_ref[0]
    logf_c = _log_sigmoid(gc)
    logf_r = _log_sigmoid(gr)
    cum_c = _cumsum_rows(keep_bf16, logf_c)
    cum_r = _cumsum_cols(logf_r, keep_bf16)
    total = jnp.sum(logf_c, axis=0, keepdims=True)

    q = q_ref[0]
    k = k_ref[0]
    v = v_ref[0]
    ones = jnp.ones((L, A_DV), BF16)
    for h in range(A_HEADS):
        qh = q[:, h * A_DK:(h + 1) * A_DK]
        kh = k[:, h * A_DK:(h + 1) * A_DK]
        vaug = jnp.concatenate([v[:, h * A_DV:(h + 1) * A_DV], ones], axis=1)
        g_c = cum_c[:, f_lane + h:f_lane + h + 1]
        i_c = gc[:, i_lane + h:i_lane + h + 1]
        g_r = cum_r[f_lane + h:f_lane + h + 1, :]
        i_r = gr[i_lane + h:i_lane + h + 1, :]
        g_tot = total[:, f_lane + h:f_lane + h + 1]
        m_prev = m_ref[h:h + 1, 0:1]

        log_d = jnp.where(keep, g_c - g_r + i_r, -jnp.inf)
        m_inter = g_c + m_prev
        m_row = jnp.maximum(jnp.max(log_d, axis=-1, keepdims=True), m_inter)
        p = (_dot_nt(qh, kh) * jnp.exp(log_d - m_row)).astype(BF16)
        inter = jnp.exp(m_inter - m_row)
        state = state_ref[h]
        nd = _dot(p, vaug) + inter * _dot(qh, state.astype(BF16))
        den = jnp.maximum(jnp.abs(nd[:, A_DV:A_DV + 1]), jnp.exp(-m_row))
        h_ref[0, :, h * A_DV:(h + 1) * A_DV] = nd[:, :A_DV] / den

        log_w = g_tot - g_c + i_c
        m_new = jnp.maximum(g_tot + m_prev, jnp.max(log_w, axis=0, keepdims=True))
        kw = (kh.astype(F32) * jnp.exp(log_w - m_new)).astype(BF16)
        state_ref[h] = jnp.exp(g_tot + m_prev - m_new) * state + _dot_tn(kw, vaug)
        m_ref[h:h + 1, :] = jnp.broadcast_to(m_new, (1, m_ref.shape[1]))

    @pl.when(step == pl.num_programs(1) - 1)
    def _():
        s_out_ref[0] = state_ref[...]
        m_out_ref[0] = m_ref[...]


def _chunk_order(n_chunks, reverse):
    return (lambda c: n_chunks - 1 - c) if reverse else (lambda c: c)


def _mlstm_scan(q, k, v, gates_c, gates_r, state0, m0, *, reverse):
    bsz, n, _ = q.shape
    L = A_CHUNK
    cidx = _chunk_order(n // L, reverse)
    sshape, mshape = (A_HEADS, A_DK, 2 * A_DV), (A_HEADS, 128)
    return pl.pallas_call(
        functools.partial(_mlstm_scan_kernel, reverse=reverse),
        grid=(bsz, n // L),
        in_specs=[pl.BlockSpec((1, L, A_QK), lambda b, c: (b, cidx(c), 0)),
                  pl.BlockSpec((1, L, A_QK), lambda b, c: (b, cidx(c), 0)),
                  pl.BlockSpec((1, L, A_V), lambda b, c: (b, cidx(c), 0)),
                  pl.BlockSpec((1, L, GATE_PAD), lambda b, c: (b, cidx(c), 0)),
                  pl.BlockSpec((1, 4 * A_HEADS, L), lambda b, c: (b, 0, cidx(c))),
                  pl.BlockSpec((1,) + sshape, lambda b, c: (b, 0, 0, 0)),
                  pl.BlockSpec((1,) + mshape, lambda b, c: (b, 0, 0))],
        out_specs=[pl.BlockSpec((1, L, A_V), lambda b, c: (b, cidx(c), 0)),
                   pl.BlockSpec((1,) + sshape, lambda b, c: (b, 0, 0, 0)),
                   pl.BlockSpec((1,) + mshape, lambda b, c: (b, 0, 0))],
        out_shape=[jax.ShapeDtypeStruct((bsz, n, A_V), F32),
                   jax.ShapeDtypeStruct((bsz,) + sshape, F32),
                   jax.ShapeDtypeStruct((bsz,) + mshape, F32)],
        scratch_shapes=[pltpu.VMEM(sshape, F32), pltpu.VMEM(mshape, F32)],
        compiler_params=_cparams(),
        name="mlstm_scan_bwd" if reverse else "mlstm_scan_fwd",
    )(q, k, v, gates_c, gates_r, state0, m0)


def _mlstm_mixer(x_ctx, x_lat, mods, ctx_row, gain, w_in, b_gate, conv_w, conv_b):
    weights = _mlstm_weights(w_in, b_gate, conv_w, conv_b)
    bsz = x_lat.shape[0]
    pc = _mlstm_proj(x_ctx, mods, ctx_row, gain, weights)
    plat = _mlstm_proj(x_lat, mods, None, gain, weights)
    zero_s = jnp.zeros((bsz, A_HEADS, A_DK, 2 * A_DV), F32)
    zero_m = jnp.zeros((bsz, A_HEADS, 128), F32)
    outs = []
    for reverse in (False, True):
        res = []
        state, m = zero_s, zero_m
        for q, k, v, _, gates in (pc, plat):
            gates_r = jnp.swapaxes(gates[:, :, :4 * A_HEADS], 1, 2)
            h, state, m = _mlstm_scan(q, k, v, gates, gates_r, state, m, reverse=reverse)
            res.append(h)
        outs.append(res)
    (hcf, hlf), (hcb, hlb) = outs
    return (hcf, hcb, pc[3]), (hlf, hlb, plat[3])


def _gla_proj_kernel(x_ref, mod_ref, gain_ref, wqkv_ref, wr_ref, wa_ref, q_ref, k_ref, v_ref, r_ref, a_ref):
    mod = mod_ref[0]
    h = _norm_mod(x_ref[0], gain_ref[...], _mod_part(mod, 1), _mod_part(mod, 0)).astype(BF16)
    qkv = _dot(h, wqkv_ref[...])
    q_ref[0] = qkv[:, :B_QK].astype(BF16)
    k_ref[0] = qkv[:, B_QK:2 * B_QK].astype(BF16)
    v_ref[0] = qkv[:, 2 * B_QK:].astype(BF16)
    r_ref[0] = _dot(h, wr_ref[...])
    a_ref[0] = _dot(h, wa_ref[...])


def _gla_proj(x, mods, cond_row, gain, weights):
    bsz, n, _ = x.shape
    rows = _block_rows(n)
    return pl.pallas_call(
        _gla_proj_kernel,
        grid=(bsz, n // rows),
        in_specs=[_row_spec(rows, D_MODEL), _mod_spec(cond_row), _resident((1, D_MODEL))]
                 + [_resident(w.shape) for w in weights],
        out_specs=[_row_spec(rows, B_QK), _row_spec(rows, B_QK), _row_spec(rows, B_V), _row_spec(rows, B_V),
                   _row_spec(rows, GATE_PAD)],
        out_shape=[jax.ShapeDtypeStruct((bsz, n, B_QK), BF16), jax.ShapeDtypeStruct((bsz, n, B_QK), BF16),
                   jax.ShapeDtypeStruct((bsz, n, B_V), BF16), jax.ShapeDtypeStruct((bsz, n, B_V), F32),
                   jax.ShapeDtypeStruct((bsz, n, GATE_PAD), F32)],
        compiler_params=_cparams(),
        name="gla_proj",
    )(x, mods, gain.reshape(1, D_MODEL), *weights)


def _gla_scan_kernel(q_ref, k_ref, v_ref, a_ref, wa_ref, ba_ref, s0_ref, o_ref, s_out_ref, state_ref, *, reverse):
    L, sub = B_CHUNK, B_SUB
    nsub = L // sub
    step = pl.program_id(1)

    @pl.when(step == 0)
    def _():
        state_ref[...] = s0_ref[0]

    row = lax.broadcasted_iota(jnp.int32, (L, L), 0)
    col = lax.broadcasted_iota(jnp.int32, (L, L), 1)
    keep = (col >= row) if reverse else (col <= row)
    keep_bf16 = jnp.where(keep, 1.0, 0.0).astype(BF16)
    cross = (col // sub > row // sub) if reverse else (col // sub < row // sub)
    row_sub = lax.broadcasted_iota(jnp.int32, (L, B_DK), 0) // sub

    pair = lax.broadcasted_iota(jnp.int32, (L * sub, 128), 0)
    lane = lax.broadcasted_iota(jnp.int32, (L * sub, 128), 1)
    pair_q, pair_s = pair // sub, pair % sub
    place = lane == (pair_q // sub) * sub + pair_s
    sel = (lax.broadcasted_iota(jnp.int32, (L, L * sub), 1) // sub
           == lax.broadcasted_iota(jnp.int32, (L, L * sub), 0))
    sel_bf16 = jnp.where(sel, 1.0, 0.0).astype(BF16)
    ones_red = jnp.ones((B_DK, 128), BF16)
    srow = lax.broadcasted_iota(jnp.int32, (sub, B_DK), 0)

    z = _dot(a_ref[0].astype(BF16), wa_ref[...]) + ba_ref[...]
    g = _log_sigmoid(z) / B_TAU
    cum = _cumsum_rows(keep_bf16, g)

    q = q_ref[0]
    k = k_ref[0]
    v = v_ref[0]
    for h in range(B_HEADS):
        G = cum[:, h * B_DK:(h + 1) * B_DK]
        qh = q[:, h * B_DK:(h + 1) * B_DK].astype(F32) * (B_DK ** -0.5)
        kh = k[:, h * B_DK:(h + 1) * B_DK].astype(F32)
        vh = v[:, h * B_DV:(h + 1) * B_DV]
        g_tot = G[0:1] if reverse else G[L - 1:L]

        state_t = state_ref[h]
        o = _dot_nt((qh * jnp.exp(G)).astype(BF16), state_t.astype(BF16))

        def edge(b):
            r = b * sub if reverse else b * sub + sub - 1
            return G[r:r + 1]
        edges = jnp.concatenate([jnp.broadcast_to(edge(b), (sub, B_DK)) for b in range(nsub)], axis=0)
        k_edge = kh * jnp.exp(edges - G)
        q_cat = jnp.concatenate(
            [(qh * jnp.exp(jnp.minimum(G - edge(b), 0.0))).astype(BF16) for b in range(nsub)], axis=1)
        k_cat = jnp.concatenate(
            [jnp.where(row_sub == b, k_edge, 0.0).astype(BF16) for b in range(nsub)], axis=1)
        att = jnp.where(cross, _dot_nt(q_cat, k_cat), 0.0)

        pieces = []
        for b in range(nsub):
            Gb = G[b * sub:(b + 1) * sub]
            kb = kh[b * sub:(b + 1) * sub]
            for j in range(sub):
                r = b * sub + j
                vis = (srow >= j) if reverse else (srow <= j)
                e = jnp.exp(jnp.minimum(G[r:r + 1] - Gb, 0.0))
                pieces.append(jnp.where(vis, qh[r:r + 1] * kb * e, 0.0))
        pair_terms = jnp.concatenate(pieces, axis=0).astype(BF16)
        pair_sum = _dot(pair_terms, ones_red)
        local = _dot(sel_bf16, jnp.where(place, pair_sum, 0.0).astype(BF16))
        att = att + local[:, :L]

        o_ref[0, :, h * B_DV:(h + 1) * B_DV] = o + _dot(att.astype(BF16), vh)
        k_tot = (kh * jnp.exp(g_tot - G)).astype(BF16)
        state_ref[h] = state_t * jnp.exp(g_tot) + _dot_tn(vh, k_tot)

    @pl.when(step == pl.num_programs(1) - 1)
    def _():
        s_out_ref[0] = state_ref[...]


def _gla_scan(q, k, v, a, wa, ba, state0, *, reverse):
    bsz, n, _ = q.shape
    L = B_CHUNK
    cidx = _chunk_order(n // L, reverse)
    sshape = (B_HEADS, B_DV, B_DK)
    return pl.pallas_call(
        functools.partial(_gla_scan_kernel, reverse=reverse),
        grid=(bsz, n // L),
        in_specs=[pl.BlockSpec((1, L, B_QK), lambda b, c: (b, cidx(c), 0)),
                  pl.BlockSpec((1, L, B_QK), lambda b, c: (b, cidx(c), 0)),
                  pl.BlockSpec((1, L, B_V), lambda b, c: (b, cidx(c), 0)),
                  pl.BlockSpec((1, L, GATE_PAD), lambda b, c: (b, cidx(c), 0)),
                  _resident(wa.shape), _resident(ba.shape),
                  pl.BlockSpec((1,) + sshape, lambda b, c: (b, 0, 0, 0))],
        out_specs=[pl.BlockSpec((1, L, B_V), lambda b, c: (b, cidx(c), 0)),
                   pl.BlockSpec((1,) + sshape, lambda b, c: (b, 0, 0, 0))],
        out_shape=[jax.ShapeDtypeStruct((bsz, n, B_V), F32), jax.ShapeDtypeStruct((bsz,) + sshape, F32)],
        scratch_shapes=[pltpu.VMEM(sshape, F32)],
        compiler_params=_cparams(),
        name="gla_scan_bwd" if reverse else "gla_scan_fwd",
    )(q, k, v, a, wa, ba, state0)


def _gla_mixer(x_ctx, x_lat, mods, ctx_row, gain, w_in, w_a2, b_a):
    wqkv = w_in[:, :2 * B_QK + B_V].astype(BF16)
    wr = w_in[:, 2 * B_QK + B_V:2 * B_QK + 2 * B_V].astype(BF16)
    wa = jnp.pad(w_in[:, 2 * B_QK + 2 * B_V:], ((0, 0), (0, GATE_PAD - 2 * B_RANK))).astype(BF16)
    bsz = x_lat.shape[0]
    pc = _gla_proj(x_ctx, mods, ctx_row, gain, (wqkv, wr, wa))
    plat = _gla_proj(x_lat, mods, None, gain, (wqkv, wr, wa))
    outs = []
    for d, reverse in enumerate((False, True)):
        wa2 = jnp.zeros((GATE_PAD, B_QK), F32).at[d * B_RANK:(d + 1) * B_RANK].set(w_a2[d]).astype(BF16)
        ba = b_a[d].reshape(1, B_QK)
        state = jnp.zeros((bsz, B_HEADS, B_DV, B_DK), F32)
        res = []
        for q, k, v, _, a in (pc, plat):
            o, state = _gla_scan(q, k, v, a, wa2, ba, state, reverse=reverse)
            res.append(o)
        outs.append(res)
    (ocf, olf), (ocb, olb) = outs
    return (ocf, ocb, pc[3]), (olf, olb, plat[3])


def _na_proj_kernel_ctx(x_ref, mod_ref, gain_ref, w_ref, wvt_ref, q_ref, k_ref, v_ref, vt_ref):
    mod = mod_ref[0]
    h = _norm_mod(x_ref[0], gain_ref[...], _mod_part(mod, 1), _mod_part(mod, 0)).astype(BF16)
    qkv = _dot(h, w_ref[...])
    q_ref[0] = (qkv[:, :D_MODEL] * (C_DH ** -0.5)).astype(BF16)
    k_ref[0] = qkv[:, D_MODEL:2 * D_MODEL].astype(BF16)
    v_ref[0] = qkv[:, 2 * D_MODEL:].astype(BF16)
    vt_ref[0] = _dot_nt(wvt_ref[...], h).astype(BF16)


def _na_proj_kernel_lat(x_ref, mod_ref, gain_ref, w_ref, wvt_ref, q_ref, k_ref, vt_ref):
    mod = mod_ref[0]
    h = _norm_mod(x_ref[0], gain_ref[...], _mod_part(mod, 1), _mod_part(mod, 0)).astype(BF16)
    qk = _dot(h, w_ref[...])
    q_ref[0] = (qk[:, :D_MODEL] * (C_DH ** -0.5)).astype(BF16)
    k_ref[0] = qk[:, D_MODEL:].astype(BF16)
    vt_ref[0] = _dot_nt(wvt_ref[...], h).astype(BF16)


def _na_proj(x, mods, cond_row, gain, w_in, *, with_v):
    bsz, n, _ = x.shape
    rows = _block_rows(n)
    wvt = w_in[:, 2 * D_MODEL:].T.astype(BF16)
    w = (w_in if with_v else w_in[:, :2 * D_MODEL]).astype(BF16)
    tok = jax.ShapeDtypeStruct((bsz, n, D_MODEL), BF16)
    vt_shape = jax.ShapeDtypeStruct((bsz, D_MODEL, n), BF16)
    vt_spec = pl.BlockSpec((1, D_MODEL, rows), lambda b, t: (b, 0, t))
    rs = _row_spec(rows, D_MODEL)
    return pl.pallas_call(
        _na_proj_kernel_ctx if with_v else _na_proj_kernel_lat,
        grid=(bsz, n // rows),
        in_specs=[rs, _mod_spec(cond_row), _resident((1, D_MODEL)), _resident(w.shape), _resident(wvt.shape)],
        out_specs=[rs, rs, rs, vt_spec] if with_v else [rs, rs, vt_spec],
        out_shape=[tok, tok, tok, vt_shape] if with_v else [tok, tok, vt_shape],
        compiler_params=_cparams(),
        name="na_proj_ctx" if with_v else "na_proj_lat",
    )(x, mods, gain.reshape(1, D_MODEL), w, wvt)


def _na_bias_table(rpb):
    colq = np.arange(GRID_W)[None, :]
    colk = np.arange(GRID_W)[:, None]
    start = np.clip(colq - C_WIN_COLS // 2, 0, GRID_W - C_WIN_COLS)
    valid = (colk >= start) & (colk < start + C_WIN_COLS)
    dc = np.clip(colk - colq + C_WIN_COLS - 1, 0, 2 * C_WIN_COLS - 2)
    bias = jnp.where(valid[None, None], rpb[:, :, dc], -jnp.inf)
    bias = bias.transpose(1, 2, 0, 3).reshape(2 * C_WIN_ROWS - 1, GRID_W, C_HEADS * GRID_W)
    return jnp.concatenate([jnp.full((1,) + bias.shape[1:], -jnp.inf, F32), bias], axis=0)


def _na_lat_kernel(q_ref, k_ref, kc_ref, vt_ref, vtc_ref, tab_ref, ot_ref, *, rows):
    r = pl.program_id(1)
    gw = C_GROUP * C_DH
    nk = C_KROWS * GRID_W
    n_ctx = kc_ref.shape[1]
    rs = jnp.clip(r - C_WIN_ROWS // 2, 0, rows - C_WIN_ROWS)
    rs_load = jnp.minimum(rs - rs % 2, rows - C_KROWS)
    start = pl.multiple_of(rs_load * GRID_W, 2 * GRID_W)
    entries = []
    for i in range(C_KROWS):
        kr = rs_load + i
        ok = jnp.logical_and(kr >= rs, kr < rs + C_WIN_ROWS)
        entries.append(jnp.where(ok, kr - r + C_WIN_ROWS, 0))

    blk_r = lax.broadcasted_iota(jnp.int32, (gw, gw), 0) // C_DH
    blk_c = lax.broadcasted_iota(jnp.int32, (gw, gw), 1) // C_DH
    same_head = blk_r == blk_c
    q = q_ref[0]
    pieces = []
    for g in range(C_HEADS // C_GROUP):
        gs = slice(g * gw, (g + 1) * gw)
        qg = q[:, gs]
        q_bd = jnp.where(same_head, jnp.concatenate([qg] * C_GROUP, axis=0), 0).astype(BF16)
        s_loc = _dot_nt(k_ref[0, pl.ds(start, nk), gs], q_bd)
        bias = jnp.concatenate([tab_ref[e, :, gs] for e in entries], axis=0)
        s_loc = s_loc + bias
        s_ctx = _dot_nt(kc_ref[0, :, gs], q_bd)
        m = jnp.maximum(jnp.max(s_loc, axis=0, keepdims=True), jnp.max(s_ctx, axis=0, keepdims=True))
        p_loc = jnp.exp(s_loc - m)
        p_ctx = jnp.exp(s_ctx - m)
        denom = jnp.sum(p_loc, axis=0, keepdims=True) + jnp.sum(p_ctx, axis=0, keepdims=True)
        ot = (_dot(vt_ref[0, gs, pl.ds(start, nk)], p_loc.astype(BF16))
              + _dot(vtc_ref[0, gs, :], p_ctx.astype(BF16)))
        ot = ot / denom
        for h in range(C_GROUP):
            hs = slice(h * C_DH, (h + 1) * C_DH)
            pieces.append(ot[hs, hs].astype(BF16))

    for half in range(2):
        @pl.when(r % 2 == half)
        def _():
            for h, piece in enumerate(pieces):
                ot_ref[0, h * C_DH:(h + 1) * C_DH, half * GRID_W:(half + 1) * GRID_W] = piece


def _na_lat_attention(q, k, kc, vt, vtc, tab):
    bsz, n, _ = q.shape
    rows = n // GRID_W
    n_ctx = kc.shape[1]
    assert rows >= C_KROWS and rows % 2 == 0
    return pl.pallas_call(
        functools.partial(_na_lat_kernel, rows=rows),
        grid=(bsz, rows),
        in_specs=[pl.BlockSpec((1, GRID_W, D_MODEL), lambda b, r: (b, r, 0)),
                  pl.BlockSpec((1, n, D_MODEL), lambda b, r: (b, 0, 0)),
                  pl.BlockSpec((1, n_ctx, D_MODEL), lambda b, r: (b, 0, 0)),
                  pl.BlockSpec((1, D_MODEL, n), lambda b, r: (b, 0, 0)),
                  pl.BlockSpec((1, D_MODEL, n_ctx), lambda b, r: (b, 0, 0)),
                  _resident(tab.shape)],
        out_specs=pl.BlockSpec((1, D_MODEL, 2 * GRID_W), lambda b, r: (b, 0, r // 2)),
        out_shape=jax.ShapeDtypeStruct((bsz, D_MODEL, n), BF16),
        compiler_params=_cparams(),
        name="na_lat_attention",
    )(q, k, kc, vt, vtc, tab)


def _na_ctx_kernel(q_ref, k_ref, v_ref, o_ref):
    for h in range(C_HEADS):
        hs = slice(h * C_DH, (h + 1) * C_DH)
        s = _dot_nt(q_ref[0, :, hs], k_ref[0, :, hs])
        p = jnp.exp(s - jnp.max(s, axis=-1, keepdims=True))
        o = _dot(p.astype(BF16), v_ref[0, :, hs]) / jnp.sum(p, axis=-1, keepdims=True)
        o_ref[0, :, hs] = o.astype(BF16)


def _na_ctx_attention(q, k, v):
    bsz, n, _ = q.shape
    spec = pl.BlockSpec((1, n, D_MODEL), lambda b: (b, 0, 0))
    return pl.pallas_call(
        _na_ctx_kernel,
        grid=(bsz,),
        in_specs=[spec, spec, spec],
        out_specs=spec,
        out_shape=jax.ShapeDtypeStruct((bsz, n, D_MODEL), BF16),
        compiler_params=_cparams(1),
        name="na_ctx_attention",
    )(q, k, v)


def kernel(x, c, ctx, c_ctx, w_ada, b_ada, norm_mix, norm_ffn, w_up, ffn_conv_w, ffn_conv_b, w_down,
           a_w_in, a_b_gate, a_conv_w, a_conv_b, a_w_out,
           b_w_in, b_w_a2, b_b_a, b_norm, b_w_out,
           c_w_in, c_rpb, c_w_out, norm_final):
    bsz = x.shape[0]
    depth = w_ada.shape[0]
    ctx_row = bsz
    assert bsz < COND_ROWS

    cond = jnp.concatenate([c, c_ctx[None], jnp.zeros((COND_ROWS - bsz - 1, D_MODEL), F32)], axis=0)
    mod_all = _ada_modulation(cond, w_ada, b_ada)
    x_ctx, x_lat = ctx, x

    for i in range(depth):
        last = i == depth - 1
        mods = mod_all[i].reshape(COND_ROWS, 1, 6 * D_MODEL)
        kind, j = i % 3, i // 3
        if kind == 0:
            acts_c, acts_l = _mlstm_mixer(x_ctx, x_lat, mods, ctx_row, norm_mix[i], a_w_in[j], a_b_gate[j],
                                          a_conv_w[j], a_conv_b[j])
            out = functools.partial(_mixer_out, _mlstm_out_kernel, "mlstm_out", consts=[], mods=mods,
                                    w_out=a_w_out[j], act_specs=[None] * 3)
        elif kind == 1:
            acts_c, acts_l = _gla_mixer(x_ctx, x_lat, mods, ctx_row, norm_mix[i], b_w_in[j], b_w_a2[j], b_b_a[j])
            out = functools.partial(_mixer_out, _gla_out_kernel, "gla_out", consts=[b_norm[j].reshape(1, B_DV)],
                                    mods=mods, w_out=b_w_out[j], act_specs=[None] * 3)
        else:
            qc, kc, vc, vtc = _na_proj(x_ctx, mods, ctx_row, norm_mix[i], c_w_in[j], with_v=True)
            ql, kl, vtl = _na_proj(x_lat, mods, None, norm_mix[i], c_w_in[j], with_v=False)
            ot = _na_lat_attention(ql, kl, kc, vtl, vtc, _na_bias_table(c_rpb[j]))
            rows = _block_rows(x_lat.shape[1])
            ot_spec = pl.BlockSpec((1, D_MODEL, rows), lambda b, t: (b, 0, t))
            mid_lat = _mixer_out(_na_out_t_kernel, "na_out_lat", x_lat, [ot_spec], [ot], [], mods, None, c_w_out[j])
            if not last:
                oc = _na_ctx_attention(qc, kc, vc)
                mid_ctx = _mixer_out(_na_out_kernel, "na_out_ctx", x_ctx, [None], [oc], [], mods, ctx_row,
                                     c_w_out[j])
        if kind != 2:
            mid_lat = out(x=x_lat, acts=list(acts_l), cond_row=None)
            if not last:
                mid_ctx = out(x=x_ctx, acts=list(acts_c), cond_row=ctx_row)
        ffn_w = _ffn_weights(w_up[i], ffn_conv_w[i], ffn_conv_b[i], w_down[i])
        x_lat = _conv_ffn(mid_lat, mods, None, norm_ffn[i], ffn_w, norm_final, final=last)
        if not last:
            x_ctx = _conv_ffn(mid_ctx, mods, ctx_row, norm_ffn[i], ffn_w, norm_final, final=False)
    return x_lat
```

```python
import functools

import jax
import jax.numpy as jnp
import numpy as np
from jax import lax
from jax.experimental import pallas as pl
from jax.experimental.pallas import tpu as pltpu

F32 = jnp.float32
BF16 = jnp.bfloat16

D_MODEL = 1024
EPS = 1e-6
GRID_W = 64

HALO = 16
VMEM_LIMIT = 56 * 1024 * 1024
COND_ROWS = 16

A_HEADS, A_DK, A_DV, A_CHUNK = 8, 64, 128, 128
A_QK, A_V = A_HEADS * A_DK, A_HEADS * A_DV
A_ONES = 16
B_HEADS, B_DK, B_DV, B_CHUNK = 4, 128, 256, 64
B_QK, B_V, B_RANK, B_TAU = B_HEADS * B_DK, B_HEADS * B_DV, 16, 16.0
B_SUB = 8
C_HEADS, C_DH, C_WIN_ROWS, C_WIN_COLS = 16, 64, 8, 16
C_GROUP = 4
C_KROWS = C_WIN_ROWS + 2
FFN_DIM = 2816
FFN_TILE = 256
GATE_PAD = 128

NT_DIMS = (((1,), (1,)), ((), ()))
TN_DIMS = (((0,), (0,)), ((), ()))


def _cparams(n_axes=2):
    return pltpu.CompilerParams(dimension_semantics=("arbitrary",) * n_axes, vmem_limit_bytes=VMEM_LIMIT)


def _resident(shape):
    nd = len(shape)
    return pl.BlockSpec(shape, lambda *_: (0,) * nd, pipeline_mode=pl.Buffered(1))


def _block_rows(n_tokens):
    return 512 if n_tokens % 512 == 0 else 256


def _dot(a, b):
    return jnp.dot(a, b, preferred_element_type=F32)


def _dot_nt(a, b):
    return lax.dot_general(a, b, NT_DIMS, preferred_element_type=F32)


def _dot_tn(a, b):
    return lax.dot_general(a, b, TN_DIMS, preferred_element_type=F32)


def _split3(x):
    x1 = x.astype(BF16)
    r1 = x - x1.astype(F32)
    x2 = r1.astype(BF16)
    x3 = (r1 - x2.astype(F32)).astype(BF16)
    return x1, x2, x3


def _cumsum_rows(mask_bf16, x):
    x1, x2, x3 = _split3(x)
    return _dot(mask_bf16, x1) + _dot(mask_bf16, x2) + _dot(mask_bf16, x3)


def _cumsum_cols(x, mask_bf16):
    x1, x2, x3 = _split3(x)
    return _dot_nt(x1, mask_bf16) + _dot_nt(x2, mask_bf16) + _dot_nt(x3, mask_bf16)


def _log_sigmoid(z):
    return jnp.minimum(z, 0.0) - jnp.log1p(jnp.exp(-jnp.abs(z)))


def _silu(z):
    return z * jax.nn.sigmoid(z)


def _rms(x, gain):
    return x * lax.rsqrt(jnp.mean(x * x, axis=-1, keepdims=True) + EPS) * gain


def _norm_mod(x, gain, scale, shift):
    return _rms(x, gain) * (1.0 + scale) + shift


def _mod_part(mod, i):
    return mod[:, i * D_MODEL:(i + 1) * D_MODEL]


def _dwconv3(u, w, b, rows):
    n = rows + 2 * HALO
    prev = pltpu.roll(u, 1, 0)[HALO:HALO + rows]
    nxt = pltpu.roll(u, n - 1, 0)[HALO:HALO + rows]
    return prev * w[0:1] + u[HALO:HALO + rows] * w[1:2] + nxt * w[2:3] + b


def _fill_halo_block(hb_ref, xp_ref, xm_ref, xn_ref, gain, scale, shift, rows):
    t = pl.program_id(1)
    has_prev = (t != 0).astype(F32)
    has_next = (t != pl.num_programs(1) - 1).astype(F32)
    hb_ref[0:HALO] = (_norm_mod(xp_ref[0], gain, scale, shift) * has_prev).astype(BF16)
    hb_ref[HALO:HALO + rows] = _norm_mod(xm_ref[0], gain, scale, shift).astype(BF16)
    hb_ref[HALO + rows:] = (_norm_mod(xn_ref[0], gain, scale, shift) * has_next).astype(BF16)


def _halo_specs(n_tokens, rows):
    per = rows // HALO
    last = n_tokens // HALO - 1
    prev = pl.BlockSpec((1, HALO, D_MODEL), lambda b, t: (b, jnp.maximum(t * per - 1, 0), 0))
    main = pl.BlockSpec((1, rows, D_MODEL), lambda b, t: (b, t, 0))
    nxt = pl.BlockSpec((1, HALO, D_MODEL), lambda b, t: (b, jnp.minimum((t + 1) * per, last), 0))
    return prev, main, nxt


def _mod_spec(cond_row):
    if cond_row is None:
        return pl.BlockSpec((1, 1, 6 * D_MODEL), lambda b, t: (b, 0, 0))
    return pl.BlockSpec((1, 1, 6 * D_MODEL), lambda b, t: (cond_row, 0, 0))


def _row_spec(rows, width):
    return pl.BlockSpec((1, rows, width), lambda b, t: (b, t, 0))


def _ada_kernel(c_ref, w_ref, b_ref, o_ref):
    s = _silu(c_ref[...]).astype(BF16)
    o_ref[0] = _dot(s, w_ref[0].astype(BF16)) + b_ref[0]


def _ada_modulation(cond, w_ada, b_ada):
    depth, _, n = w_ada.shape
    tn = 1024
    return pl.pallas_call(
        _ada_kernel,
        grid=(depth, n // tn),
        in_specs=[pl.BlockSpec((COND_ROWS, D_MODEL), lambda l, j: (0, 0)),
                  pl.BlockSpec((1, D_MODEL, tn), lambda l, j: (l, 0, j)),
                  pl.BlockSpec((1, 1, tn), lambda l, j: (l, 0, j))],
        out_specs=pl.BlockSpec((1, COND_ROWS, tn), lambda l, j: (l, 0, j)),
        out_shape=jax.ShapeDtypeStruct((depth, COND_ROWS, n), F32),
        compiler_params=_cparams(),
        name="ada_modulation",
    )(cond, w_ada, b_ada.reshape(depth, 1, n))


def _ffn_kernel(xp_ref, xm_ref, xn_ref, mod_ref, gain_ref, wup_ref, cw_ref, cb_ref, wdn_ref, gfin_ref,
                out_ref, hb_ref, z_ref, *, rows, final):
    mod = mod_ref[0]
    shift, scale, gate = _mod_part(mod, 3), _mod_part(mod, 4), _mod_part(mod, 5)
    _fill_halo_block(hb_ref, xp_ref, xm_ref, xn_ref, gain_ref[...], scale, shift, rows)
    for c in range(FFN_DIM // FFN_TILE):
        a = _dwconv3(_dot(hb_ref[...], wup_ref[0, c]), cw_ref[0, c], cb_ref[0, c], rows)
        g = _dwconv3(_dot(hb_ref[...], wup_ref[1, c]), cw_ref[1, c], cb_ref[1, c], rows)
        z_ref[:, c * FFN_TILE:(c + 1) * FFN_TILE] = (_silu(g) * a).astype(BF16)
    y = xm_ref[0] + gate * _dot(z_ref[...], wdn_ref[...])
    if final:
        y = _rms(y, gfin_ref[...])
    out_ref[0] = y


def _ffn_weights(w_up, conv_w, conv_b, w_down):
    nt = FFN_DIM // FFN_TILE
    wup = w_up.astype(BF16).reshape(D_MODEL, 2, nt, FFN_TILE).transpose(1, 2, 0, 3)
    cw = conv_w.reshape(3, 2, nt, FFN_TILE).transpose(1, 2, 0, 3)
    cb = conv_b.reshape(2, nt, 1, FFN_TILE)
    return wup, cw, cb, w_down.astype(BF16)


def _conv_ffn(x, mods, cond_row, gain, weights, gain_final, *, final):
    bsz, n, _ = x.shape
    rows = _block_rows(n)
    wup, cw, cb, wdn = weights
    prev, main, nxt = _halo_specs(n, rows)
    return pl.pallas_call(
        functools.partial(_ffn_kernel, rows=rows, final=final),
        grid=(bsz, n // rows),
        in_specs=[prev, main, nxt, _mod_spec(cond_row), _resident((1, D_MODEL)),
                  _resident(wup.shape), _resident(cw.shape), _resident(cb.shape), _resident(wdn.shape),
                  _resident((1, D_MODEL))],
        out_specs=_row_spec(rows, D_MODEL),
        out_shape=jax.ShapeDtypeStruct((bsz, n, D_MODEL), F32),
        scratch_shapes=[pltpu.VMEM((rows + 2 * HALO, D_MODEL), BF16),
                        pltpu.VMEM((rows, FFN_DIM), BF16)],
        compiler_params=_cparams(),
        name="conv_ffn",
    )(x, x, x, mods, gain.reshape(1, D_MODEL), wup, cw, cb, wdn, gain_final.reshape(1, D_MODEL))


def _mlstm_out_kernel(x_ref, hft_ref, hbt_ref, ot_ref, mod_ref, w_ref, out_ref):
    zt = ((hft_ref[0] + hbt_ref[0]) * jax.nn.sigmoid(ot_ref[0])).astype(BF16)
    out_ref[0] = x_ref[0] + _mod_part(mod_ref[0], 2) * _dot_tn(zt, w_ref[...])


def _gla_out_kernel(x_ref, of_ref, ob_ref, r_ref, gn_ref, mod_ref, w_ref, out_ref):
    oo = of_ref[0] + ob_ref[0]
    parts = [_rms(oo[:, h * B_DV:(h + 1) * B_DV], gn_ref[...]) for h in range(B_HEADS)]
    z = (jnp.concatenate(parts, axis=1) * _silu(r_ref[0])).astype(BF16)
    out_ref[0] = x_ref[0] + _mod_part(mod_ref[0], 2) * _dot(z, w_ref[...])


def _na_out_kernel(x_ref, o_ref, mod_ref, w_ref, out_ref):
    out_ref[0] = x_ref[0] + _mod_part(mod_ref[0], 2) * _dot(o_ref[0], w_ref[...])


def _na_out_t_kernel(x_ref, ot_ref, mod_ref, w_ref, out_ref):
    out_ref[0] = x_ref[0] + _mod_part(mod_ref[0], 2) * _dot_tn(ot_ref[0], w_ref[...])


def _mixer_out(kernel_fn, name, x, act_specs, acts, consts, mods, cond_row, w_out):
    bsz, n, _ = x.shape
    rows = _block_rows(n)
    w = w_out.astype(BF16)
    specs = [pl.BlockSpec((1, a.shape[1], rows), lambda b, t: (b, 0, t)) if s == "t"
             else _row_spec(rows, a.shape[-1]) for s, a in zip(act_specs, acts)]
    return pl.pallas_call(
        kernel_fn,
        grid=(bsz, n // rows),
        in_specs=([_row_spec(rows, D_MODEL)] + specs + [_resident(c.shape) for c in consts]
                  + [_mod_spec(cond_row), _resident(w.shape)]),
        out_specs=_row_spec(rows, D_MODEL),
        out_shape=jax.ShapeDtypeStruct((bsz, n, D_MODEL), F32),
        compiler_params=_cparams(),
        name=name,
    )(x, *acts, *consts, mods, w)


def _mlstm_proj_kernel(xp_ref, xm_ref, xn_ref, mod_ref, gain_ref, wqk_ref, cw_ref, cb_ref, wvot_ref, wg_ref,
                       bg_ref, wgt_ref, bgt_ref, q_ref, k_ref, vt_ref, ot_ref, bc_ref, gr_ref, bm_ref, hb_ref,
                       *, rows):
    mod = mod_ref[0]
    shift, scale = _mod_part(mod, 0), _mod_part(mod, 1)
    _fill_halo_block(hb_ref, xp_ref, xm_ref, xn_ref, gain_ref[...], scale, shift, rows)
    qk = _silu(_dwconv3(_dot(hb_ref[...], wqk_ref[...]), cw_ref[...], cb_ref[...], rows))
    q_ref[0] = (qk[:, :A_QK] * (A_DK ** -0.5)).astype(BF16)
    k_ref[0] = qk[:, A_QK:].astype(BF16)
    hm = hb_ref[HALO:HALO + rows]
    vot = _dot_nt(wvot_ref[...], hm)
    vt_ref[0] = vot[:A_V].astype(BF16)
    ot_ref[0] = vot[A_V:]
    L, H = A_CHUNK, A_HEADS
    gates_c = _dot(hm, wg_ref[...]) + bg_ref[...]
    gates_r = _dot_nt(wgt_ref[...], hm) + bgt_ref[...]
    key = lax.broadcasted_iota(jnp.int32, (L, L), 0)
    qry = lax.broadcasted_iota(jnp.int32, (L, L), 1)
    fwd_mask = jnp.where(qry <= key, 1.0, 0.0).astype(BF16)
    bwd_mask = jnp.where(qry >= key, 1.0, 0.0).astype(BF16)
    fwd_lane = lax.broadcasted_iota(jnp.int32, (L, GATE_PAD), 1) < H
    fwd_row = lax.broadcasted_iota(jnp.int32, (2 * H, L), 0) < H
    lane = lax.broadcasted_iota(jnp.int32, (2 * H, L), 1)
    for c in range(rows // L):
        cs = slice(c * L, (c + 1) * L)
        logf_c = _log_sigmoid(gates_c[cs, GATE_PAD:])
        g_c = jnp.where(fwd_lane, _cumsum_rows(fwd_mask, logf_c), _cumsum_rows(bwd_mask, logf_c))
        bc_ref[0, cs, :] = gates_c[cs, :GATE_PAD] - g_c
        logf_r = _log_sigmoid(gates_r[2 * H:, cs])
        g_r = jnp.where(fwd_row, _cumsum_cols(logf_r, fwd_mask), _cumsum_cols(logf_r, bwd_mask))
        gr_ref[0, :, cs] = g_r
        run_max = gates_r[:2 * H, cs] - g_r
        dist = 1
        while dist < L:
            seen_f = jnp.where(lane >= dist, pltpu.roll(run_max, dist, 1), -jnp.inf)
            seen_b = jnp.where(lane < L - dist, pltpu.roll(run_max, L - dist, 1), -jnp.inf)
            run_max = jnp.maximum(run_max, jnp.where(fwd_row, seen_f, seen_b))
            dist *= 2
        bm_ref[0, :, cs] = run_max


def _mlstm_weights(w_in, b_gate, conv_w, conv_b):
    hh = A_HEADS
    wqk = w_in[:, :2 * A_QK].astype(BF16)
    wvot = w_in[:, 2 * A_QK:2 * A_QK + 2 * A_V].T.astype(BF16)
    wgate = w_in[:, 2 * A_QK + 2 * A_V:]
    pad = GATE_PAD - 2 * hh

    def regroup(t, axis, pad):
        i_f, f_f, i_b, f_b = jnp.split(t, 4, axis=axis)
        widths = [(0, 0)] * (t.ndim - 1) + [(0, pad)]
        return jnp.concatenate([jnp.pad(jnp.concatenate([i_f, i_b], axis), widths),
                                jnp.pad(jnp.concatenate([f_f, f_b], axis), widths)], axis)

    wg = regroup(wgate, 1, pad).astype(BF16)
    bg = regroup(b_gate, 0, pad).reshape(1, 2 * GATE_PAD)
    wgt = regroup(wgate, 1, 0).T.astype(BF16)
    bgt = regroup(b_gate, 0, 0).reshape(4 * hh, 1)
    return wqk, conv_w, conv_b.reshape(1, -1), wvot, wg, bg, wgt, bgt


def _mlstm_proj(x, mods, cond_row, gain, weights):
    bsz, n, _ = x.shape
    rows = _block_rows(n)
    prev, main, nxt = _halo_specs(n, rows)
    t_spec = pl.BlockSpec((1, A_V, rows), lambda b, t: (b, 0, t))
    g_spec = pl.BlockSpec((1, 2 * A_HEADS, rows), lambda b, t: (b, 0, t))
    g_shape = jax.ShapeDtypeStruct((bsz, 2 * A_HEADS, n), F32)
    return pl.pallas_call(
        functools.partial(_mlstm_proj_kernel, rows=rows),
        grid=(bsz, n // rows),
        in_specs=[prev, main, nxt, _mod_spec(cond_row), _resident((1, D_MODEL))]
                 + [_resident(w.shape) for w in weights],
        out_specs=[_row_spec(rows, A_QK), _row_spec(rows, A_QK), t_spec, t_spec,
                   _row_spec(rows, GATE_PAD), g_spec, g_spec],
        out_shape=[jax.ShapeDtypeStruct((bsz, n, A_QK), BF16), jax.ShapeDtypeStruct((bsz, n, A_QK), BF16),
                   jax.ShapeDtypeStruct((bsz, A_V, n), BF16), jax.ShapeDtypeStruct((bsz, A_V, n), F32),
                   jax.ShapeDtypeStruct((bsz, n, GATE_PAD), F32), g_shape, g_shape],
        scratch_shapes=[pltpu.VMEM((rows + 2 * HALO, D_MODEL), BF16)],
        compiler_params=_cparams(),
        name="mlstm_proj",
    )(x, x, x, mods, gain.reshape(1, D_MODEL), *weights)


def _mlstm_scan_kernel(q_ref, k_ref, vt_ref, bc_ref, gr_ref, bm_ref, s0_ref, m0_ref,
                       ht_ref, s_out_ref, m_out_ref, state_ref, m_ref, *, reverse):
    L, H = A_CHUNK, A_HEADS
    step = pl.program_id(1)

    @pl.when(step == 0)
    def _():
        state_ref[...] = s0_ref[0]
        m_ref[...] = m0_ref[0]

    key = lax.broadcasted_iota(jnp.int32, (L, L), 0)
    qry = lax.broadcasted_iota(jnp.int32, (L, L), 1)
    keep_t = (key >= qry) if reverse else (key <= qry)
    lane0 = H if reverse else 0
    end = 0 if reverse else L - 1

    b_c = bc_ref[0]
    cum_r = gr_ref[0, lane0:lane0 + H, :]
    run_max = bm_ref[0, lane0:lane0 + H, :]

    m_prev = m_ref[...]
    mx = jnp.maximum(run_max, m_prev)
    mx_end = jnp.broadcast_to(mx[:, end:end + 1], m_prev.shape)
    g_tot = jnp.broadcast_to(cum_r[:, end:end + 1], m_prev.shape)
    inter_all = jnp.exp(m_prev - mx)
    floor_all = jnp.exp(-(cum_r + mx))
    decay_all = jnp.exp(m_prev - mx_end)
    m_ref[...] = g_tot + mx_end

    head_sub = lax.broadcasted_iota(jnp.int32, m_prev.shape, 0)
    head_lane = lax.broadcasted_iota(jnp.int32, m_prev.shape, 1)
    mx_end_lanes = jnp.sum(jnp.where(head_lane == head_sub + lane0, mx_end, 0.0), axis=0, keepdims=True)
    w_c = jnp.exp(b_c - mx_end_lanes)

    heads = range(H)
    ones = jnp.ones((A_ONES, L), BF16)
    qs = [q_ref[0, :, h * A_DK:(h + 1) * A_DK] for h in heads]
    ks = [k_ref[0, :, h * A_DK:(h + 1) * A_DK] for h in heads]
    vaug_t = [jnp.concatenate([vt_ref[0, h * A_DV:(h + 1) * A_DV, :], ones], axis=0) for h in heads]
    states = [state_ref[h] for h in heads]
    s_t = [_dot_nt(ks[h], qs[h]) for h in heads]
    carry_t = [_dot_nt(states[h].astype(BF16), qs[h]) for h in heads]
    kw = [(ks[h].astype(F32) * w_c[:, lane0 + h:lane0 + h + 1]).astype(BF16) for h in heads]
    for h in heads:
        state_ref[h] = decay_all[h:h + 1, 0:1] * states[h] + _dot(vaug_t[h], kw[h])
    p_t = []
    for h in heads:
        b_col = b_c[:, lane0 + h:lane0 + h + 1]
        w_t = jnp.exp(jnp.where(keep_t, b_col - mx[h:h + 1, :], -jnp.inf))
        p_t.append((s_t[h] * w_t).astype(BF16))
    for h in heads:
        tot = _dot(vaug_t[h], p_t[h]) + inter_all[h:h + 1, :] * carry_t[h]
        den = jnp.maximum(jnp.abs(tot[A_DV:A_DV + 1, :]), floor_all[h:h + 1, :])
        ht_ref[0, h * A_DV:(h + 1) * A_DV, :] = tot[:A_DV] / den

    @pl.when(step == pl.num_programs(1) - 1)
    def _():
        s_out_ref[0] = state_ref[...]
        m_out_ref[0] = m_ref[...]


def _chunk_order(n_chunks, reverse):
    return (lambda c: n_chunks - 1 - c) if reverse else (lambda c: c)


A_STATE = (A_HEADS, A_DV + A_ONES, A_DK)
A_MSTATE = (A_HEADS, 128)


def _mlstm_scan(q, k, vt, b_c, g_r, bmax_r, state0, m0, *, reverse):
    bsz, n, _ = q.shape
    L = A_CHUNK
    cidx = _chunk_order(n // L, reverse)
    tok = lambda w: pl.BlockSpec((1, L, w), lambda b, c: (b, cidx(c), 0))
    chan = lambda r: pl.BlockSpec((1, r, L), lambda b, c: (b, 0, cidx(c)))
    s_spec = pl.BlockSpec((1,) + A_STATE, lambda b, c: (b, 0, 0, 0))
    m_spec = pl.BlockSpec((1,) + A_MSTATE, lambda b, c: (b, 0, 0))
    return pl.pallas_call(
        functools.partial(_mlstm_scan_kernel, reverse=reverse),
        grid=(bsz, n // L),
        in_specs=[tok(A_QK), tok(A_QK), chan(A_V), tok(GATE_PAD),
                  chan(2 * A_HEADS), chan(2 * A_HEADS), s_spec, m_spec],
        out_specs=[chan(A_V), s_spec, m_spec],
        out_shape=[jax.ShapeDtypeStruct((bsz, A_V, n), F32),
                   jax.ShapeDtypeStruct((bsz,) + A_STATE, F32),
                   jax.ShapeDtypeStruct((bsz,) + A_MSTATE, F32)],
        scratch_shapes=[pltpu.VMEM(A_STATE, F32), pltpu.VMEM(A_MSTATE, F32)],
        compiler_params=_cparams(),
        name="mlstm_scan_bwd" if reverse else "mlstm_scan_fwd",
    )(q, k, vt, b_c, g_r, bmax_r, state0, m0)


def _mlstm_mixer(x_ctx, x_lat, mods, ctx_row, gain, w_in, b_gate, conv_w, conv_b):
    weights = _mlstm_weights(w_in, b_gate, conv_w, conv_b)
    bsz = x_lat.shape[0]
    streams = []
    for x, row in ((x_ctx, ctx_row), (x_lat, None)):
        q, k, vt, ot, b_c, g_r, bmax_r = _mlstm_proj(x, mods, row, gain, weights)
        streams.append(((q, k, vt, b_c, g_r, bmax_r), ot))
    outs = []
    for reverse in (False, True):
        state = jnp.zeros((bsz,) + A_STATE, F32)
        m = jnp.zeros((bsz,) + A_MSTATE, F32)
        res = []
        for args, _ in streams:
            ht, state, m = _mlstm_scan(*args, state, m, reverse=reverse)
            res.append(ht)
        outs.append(res)
    (hcf, hlf), (hcb, hlb) = outs
    return (hcf, hcb, streams[0][1]), (hlf, hlb, streams[1][1])


def _gla_proj_kernel(x_ref, mod_ref, gain_ref, wqkv_ref, wr_ref, wa_ref, wa2_ref, ba_ref,
                     q_ref, k_ref, v_ref, r_ref, gf_ref, gb_ref, *, rows):
    mod = mod_ref[0]
    h = _norm_mod(x_ref[0], gain_ref[...], _mod_part(mod, 1), _mod_part(mod, 0)).astype(BF16)
    qkv = _dot(h, wqkv_ref[...])
    q_ref[0] = qkv[:, :B_QK].astype(BF16)
    k_ref[0] = qkv[:, B_QK:2 * B_QK].astype(BF16)
    v_ref[0] = qkv[:, 2 * B_QK:].astype(BF16)
    r_ref[0] = _dot(h, wr_ref[...])
    L = B_CHUNK
    a = _dot(h, wa_ref[...]).astype(BF16)
    row = lax.broadcasted_iota(jnp.int32, (L, L), 0)
    col = lax.broadcasted_iota(jnp.int32, (L, L), 1)
    for d, out_ref in enumerate((gf_ref, gb_ref)):
        mask = jnp.where((col >= row) if d else (col <= row), 1.0, 0.0).astype(BF16)
        g = _log_sigmoid(_dot(a, wa2_ref[d]) + ba_ref[d]) / B_TAU
        for c in range(rows // L):
            out_ref[0, c * L:(c + 1) * L, :] = _cumsum_rows(mask, g[c * L:(c + 1) * L])


def _gla_proj(x, mods, cond_row, gain, weights):
    bsz, n, _ = x.shape
    rows = _block_rows(n)
    return pl.pallas_call(
        functools.partial(_gla_proj_kernel, rows=rows),
        grid=(bsz, n // rows),
        in_specs=[_row_spec(rows, D_MODEL), _mod_spec(cond_row), _resident((1, D_MODEL))]
                 + [_resident(w.shape) for w in weights],
        out_specs=[_row_spec(rows, B_QK), _row_spec(rows, B_QK), _row_spec(rows, B_V), _row_spec(rows, B_V),
                   _row_spec(rows, B_QK), _row_spec(rows, B_QK)],
        out_shape=[jax.ShapeDtypeStruct((bsz, n, B_QK), BF16), jax.ShapeDtypeStruct((bsz, n, B_QK), BF16),
                   jax.ShapeDtypeStruct((bsz, n, B_V), BF16), jax.ShapeDtypeStruct((bsz, n, B_V), F32),
                   jax.ShapeDtypeStruct((bsz, n, B_QK), F32), jax.ShapeDtypeStruct((bsz, n, B_QK), F32)],
        compiler_params=_cparams(),
        name="gla_proj",
    )(x, mods, gain.reshape(1, D_MODEL), *weights)


def _gla_scan_kernel(q_ref, k_ref, v_ref, g_ref, s0_ref, o_ref, s_out_ref, state_ref, *, reverse):
    L, sub = B_CHUNK, B_SUB
    nsub = L // sub
    step = pl.program_id(1)

    @pl.when(step == 0)
    def _():
        state_ref[...] = s0_ref[0]

    pair = lax.broadcasted_iota(jnp.int32, (L * sub, 128), 0)
    lane = lax.broadcasted_iota(jnp.int32, (L * sub, 128), 1)
    pair_q, pair_s = pair // sub, pair % sub
    place = lane == (pair_q // sub) * sub + pair_s
    sel = (lax.broadcasted_iota(jnp.int32, (L, L * sub), 1) // sub
           == lax.broadcasted_iota(jnp.int32, (L, L * sub), 0))
    sel_bf16 = jnp.where(sel, 1.0, 0.0).astype(BF16)
    ones_red = jnp.ones((B_DK, 128), BF16)
    srow = lax.broadcasted_iota(jnp.int32, (sub, B_DK), 0)

    cum = g_ref[0]

    heads = range(B_HEADS)
    Gs = [cum[:, h * B_DK:(h + 1) * B_DK] for h in heads]
    qs = [q_ref[0, :, h * B_DK:(h + 1) * B_DK].astype(F32) * (B_DK ** -0.5) for h in heads]
    ks = [k_ref[0, :, h * B_DK:(h + 1) * B_DK].astype(F32) for h in heads]
    vs = [v_ref[0, :, h * B_DV:(h + 1) * B_DV] for h in heads]
    g_tot = [(G[0:1] if reverse else G[L - 1:L]) for G in Gs]

    states = [state_ref[h] for h in heads]
    o_carry = [_dot_nt((qs[h] * jnp.exp(Gs[h])).astype(BF16), states[h].astype(BF16)) for h in heads]
    for h in heads:
        k_tot = (ks[h] * jnp.exp(g_tot[h] - Gs[h])).astype(BF16)
        state_ref[h] = states[h] * jnp.exp(g_tot[h]) + _dot_tn(vs[h], k_tot)

    def cross_scores(G, qh, kh):
        def edge(b):
            r = b * sub if reverse else b * sub + sub - 1
            return G[r:r + 1]
        edges = jnp.concatenate([jnp.broadcast_to(edge(b), (sub, B_DK)) for b in range(nsub)], axis=0)
        k_edge = kh * jnp.exp(edges - G)
        q_cols, k_cols = [], []
        for b in range(nsub):
            lo, hi = b * sub, (b + 1) * sub
            before, after = jnp.zeros((lo, B_DK), F32), jnp.zeros((L - hi, B_DK), F32)
            if reverse:
                seen = [qh[:lo] * jnp.exp(G[:lo] - edge(b)), jnp.zeros((L - lo, B_DK), F32)]
            else:
                seen = [jnp.zeros((hi, B_DK), F32), qh[hi:] * jnp.exp(G[hi:] - edge(b))]
            q_cols.append(jnp.concatenate([s for s in seen if s.shape[0]], axis=0).astype(BF16))
            k_parts = [before, k_edge[lo:hi], after]
            k_cols.append(jnp.concatenate([s for s in k_parts if s.shape[0]], axis=0).astype(BF16))
        return _dot_nt(jnp.concatenate(q_cols, axis=1), jnp.concatenate(k_cols, axis=1))

    att_cross = [cross_scores(Gs[h], qs[h], ks[h]) for h in heads]

    def pair_sums(G, qh, kh):
        pieces = []
        for b in range(nsub):
            Gb = G[b * sub:(b + 1) * sub]
            kb = kh[b * sub:(b + 1) * sub]
            for j in range(sub):
                r = b * sub + j
                vis = (srow >= j) if reverse else (srow <= j)
                pieces.append(qh[r:r + 1] * kb * jnp.exp(jnp.where(vis, G[r:r + 1] - Gb, -jnp.inf)))
        pair_terms = jnp.concatenate(pieces, axis=0).astype(BF16)
        return _dot(pair_terms, ones_red)

    sums = [pair_sums(Gs[h], qs[h], ks[h]) for h in heads]
    att_local = [_dot(sel_bf16, jnp.where(place, sums[h], 0.0).astype(BF16)) for h in heads]
    for h in heads:
        att = (att_cross[h] + att_local[h][:, :L]).astype(BF16)
        o_ref[0, :, h * B_DV:(h + 1) * B_DV] = o_carry[h] + _dot(att, vs[h])

    @pl.when(step == pl.num_programs(1) - 1)
    def _():
        s_out_ref[0] = state_ref[...]


def _gla_scan(q, k, v, g, state0, *, reverse):
    bsz, n, _ = q.shape
    L = B_CHUNK
    cidx = _chunk_order(n // L, reverse)
    sshape = (B_HEADS, B_DV, B_DK)
    return pl.pallas_call(
        functools.partial(_gla_scan_kernel, reverse=reverse),
        grid=(bsz, n // L),
        in_specs=[pl.BlockSpec((1, L, B_QK), lambda b, c: (b, cidx(c), 0)),
                  pl.BlockSpec((1, L, B_QK), lambda b, c: (b, cidx(c), 0)),
                  pl.BlockSpec((1, L, B_V), lambda b, c: (b, cidx(c), 0)),
                  pl.BlockSpec((1, L, B_QK), lambda b, c: (b, cidx(c), 0)),
                  pl.BlockSpec((1,) + sshape, lambda b, c: (b, 0, 0, 0))],
        out_specs=[pl.BlockSpec((1, L, B_V), lambda b, c: (b, cidx(c), 0)),
                   pl.BlockSpec((1,) + sshape, lambda b, c: (b, 0, 0, 0))],
        out_shape=[jax.ShapeDtypeStruct((bsz, n, B_V), F32), jax.ShapeDtypeStruct((bsz,) + sshape, F32)],
        scratch_shapes=[pltpu.VMEM(sshape, F32)],
        compiler_params=_cparams(),
        name="gla_scan_bwd" if reverse else "gla_scan_fwd",
    )(q, k, v, g, state0)


def _gla_mixer(x_ctx, x_lat, mods, ctx_row, gain, w_in, w_a2, b_a):
    wqkv = w_in[:, :2 * B_QK + B_V].astype(BF16)
    wr = w_in[:, 2 * B_QK + B_V:2 * B_QK + 2 * B_V].astype(BF16)
    wa = jnp.pad(w_in[:, 2 * B_QK + 2 * B_V:], ((0, 0), (0, GATE_PAD - 2 * B_RANK))).astype(BF16)
    wa2 = jnp.stack([jnp.zeros((GATE_PAD, B_QK), F32).at[d * B_RANK:(d + 1) * B_RANK].set(w_a2[d])
                     for d in range(2)]).astype(BF16)
    ba = b_a.reshape(2, 1, B_QK)
    bsz = x_lat.shape[0]
    pc = _gla_proj(x_ctx, mods, ctx_row, gain, (wqkv, wr, wa, wa2, ba))
    plat = _gla_proj(x_lat, mods, None, gain, (wqkv, wr, wa, wa2, ba))
    outs = []
    for d, reverse in enumerate((False, True)):
        state = jnp.zeros((bsz, B_HEADS, B_DV, B_DK), F32)
        res = []
        for p in (pc, plat):
            o, state = _gla_scan(p[0], p[1], p[2], p[4 + d], state, reverse=reverse)
            res.append(o)
        outs.append(res)
    (ocf, olf), (ocb, olb) = outs
    return (ocf, ocb, pc[3]), (olf, olb, plat[3])


def _na_proj_kernel_ctx(x_ref, mod_ref, gain_ref, w_ref, wvt_ref, q_ref, k_ref, v_ref, vt_ref):
    mod = mod_ref[0]
    h = _norm_mod(x_ref[0], gain_ref[...], _mod_part(mod, 1), _mod_part(mod, 0)).astype(BF16)
    qkv = _dot(h, w_ref[...])
    q_ref[0] = (qkv[:, :D_MODEL] * (C_DH ** -0.5)).astype(BF16)
    k_ref[0] = qkv[:, D_MODEL:2 * D_MODEL].astype(BF16)
    v_ref[0] = qkv[:, 2 * D_MODEL:].astype(BF16)
    vt_ref[0] = _dot_nt(wvt_ref[...], h).astype(BF16)


def _na_proj_kernel_lat(x_ref, mod_ref, gain_ref, w_ref, wvt_ref, q_ref, k_ref, vt_ref):
    mod = mod_ref[0]
    h = _norm_mod(x_ref[0], gain_ref[...], _mod_part(mod, 1), _mod_part(mod, 0)).astype(BF16)
    qk = _dot(h, w_ref[...])
    q_ref[0] = (qk[:, :D_MODEL] * (C_DH ** -0.5)).astype(BF16)
    k_ref[0] = qk[:, D_MODEL:].astype(BF16)
    vt_ref[0] = _dot_nt(wvt_ref[...], h).astype(BF16)


def _na_proj(x, mods, cond_row, gain, w_in, *, with_v):
    bsz, n, _ = x.shape
    rows = _block_rows(n)
    wvt = w_in[:, 2 * D_MODEL:].T.astype(BF16)
    w = (w_in if with_v else w_in[:, :2 * D_MODEL]).astype(BF16)
    tok = jax.ShapeDtypeStruct((bsz, n, D_MODEL), BF16)
    vt_shape = jax.ShapeDtypeStruct((bsz, D_MODEL, n), BF16)
    vt_spec = pl.BlockSpec((1, D_MODEL, rows), lambda b, t: (b, 0, t))
    rs = _row_spec(rows, D_MODEL)
    return pl.pallas_call(
        _na_proj_kernel_ctx if with_v else _na_proj_kernel_lat,
        grid=(bsz, n // rows),
        in_specs=[rs, _mod_spec(cond_row), _resident((1, D_MODEL)), _resident(w.shape), _resident(wvt.shape)],
        out_specs=[rs, rs, rs, vt_spec] if with_v else [rs, rs, vt_spec],
        out_shape=[tok, tok, tok, vt_shape] if with_v else [tok, tok, vt_shape],
        compiler_params=_cparams(),
        name="na_proj_ctx" if with_v else "na_proj_lat",
    )(x, mods, gain.reshape(1, D_MODEL), w, wvt)


def _na_bias_table(rpb):
    colq = np.arange(GRID_W)[None, :]
    colk = np.arange(GRID_W)[:, None]
    start = np.clip(colq - C_WIN_COLS // 2, 0, GRID_W - C_WIN_COLS)
    valid = (colk >= start) & (colk < start + C_WIN_COLS)
    dc = np.clip(colk - colq + C_WIN_COLS - 1, 0, 2 * C_WIN_COLS - 2)
    bias = jnp.where(valid[None, None], rpb[:, :, dc], -jnp.inf)
    bias = bias.transpose(1, 2, 0, 3).reshape(2 * C_WIN_ROWS - 1, GRID_W, C_HEADS * GRID_W)
    return jnp.concatenate([jnp.full((1,) + bias.shape[1:], -jnp.inf, F32), bias], axis=0)


def _na_lat_kernel(q_ref, k_ref, kc_ref, vt_ref, vtc_ref, tab_ref, ot_ref, *, rows):
    r = pl.program_id(1)
    gw = C_GROUP * C_DH
    nk = C_KROWS * GRID_W
    n_ctx = kc_ref.shape[1]
    rs = jnp.clip(r - C_WIN_ROWS // 2, 0, rows - C_WIN_ROWS)
    rs_load = jnp.minimum(rs - rs % 2, rows - C_KROWS)
    start = pl.multiple_of(rs_load * GRID_W, 2 * GRID_W)
    entries = []
    for i in range(C_KROWS):
        kr = rs_load + i
        ok = jnp.logical_and(kr >= rs, kr < rs + C_WIN_ROWS)
        entries.append(jnp.where(ok, kr - r + C_WIN_ROWS, 0))

    blk_r = lax.broadcasted_iota(jnp.int32, (gw, gw), 0) // C_DH
    blk_c = lax.broadcasted_iota(jnp.int32, (gw, gw), 1) // C_DH
    same_head = blk_r == blk_c
    groups = range(C_HEADS // C_GROUP)
    gsl = [slice(g * gw, (g + 1) * gw) for g in groups]
    q_bd = [jnp.where(same_head, jnp.concatenate([q_ref[0, :, gs]] * C_GROUP, axis=0), 0).astype(BF16)
            for gs in gsl]
    s_loc = [_dot_nt(k_ref[0, pl.ds(start, nk), gs], qb) for gs, qb in zip(gsl, q_bd)]
    s_ctx = [_dot_nt(kc_ref[0, :, gs], qb) for gs, qb in zip(gsl, q_bd)]
    p_loc, p_ctx, denom = [], [], []
    for g in groups:
        sl = s_loc[g] + jnp.concatenate([tab_ref[e, :, gsl[g]] for e in entries], axis=0)
        m = jnp.maximum(jnp.max(sl, axis=0, keepdims=True), jnp.max(s_ctx[g], axis=0, keepdims=True))
        pl_, pc_ = jnp.exp(sl - m), jnp.exp(s_ctx[g] - m)
        denom.append(jnp.sum(pl_, axis=0, keepdims=True) + jnp.sum(pc_, axis=0, keepdims=True))
        p_loc.append(pl_.astype(BF16))
        p_ctx.append(pc_.astype(BF16))
    pieces = []
    for g in groups:
        ot = (_dot(vt_ref[0, gsl[g], pl.ds(start, nk)], p_loc[g])
              + _dot(vtc_ref[0, gsl[g], :], p_ctx[g])) / denom[g]
        for h in range(C_GROUP):
            hs = slice(h * C_DH, (h + 1) * C_DH)
            pieces.append(ot[hs, hs].astype(BF16))

    for half in range(2):
        @pl.when(r % 2 == half)
        def _():
            for h, piece in enumerate(pieces):
                ot_ref[0, h * C_DH:(h + 1) * C_DH, half * GRID_W:(half + 1) * GRID_W] = piece


def _na_lat_attention(q, k, kc, vt, vtc, tab):
    bsz, n, _ = q.shape
    rows = n // GRID_W
    n_ctx = kc.shape[1]
    assert rows >= C_KROWS and rows % 2 == 0
    return pl.pallas_call(
        functools.partial(_na_lat_kernel, rows=rows),
        grid=(bsz, rows),
        in_specs=[pl.BlockSpec((1, GRID_W, D_MODEL), lambda b, r: (b, r, 0)),
                  pl.BlockSpec((1, n, D_MODEL), lambda b, r: (b, 0, 0)),
                  pl.BlockSpec((1, n_ctx, D_MODEL), lambda b, r: (b, 0, 0)),
                  pl.BlockSpec((1, D_MODEL, n), lambda b, r: (b, 0, 0)),
                  pl.BlockSpec((1, D_MODEL, n_ctx), lambda b, r: (b, 0, 0)),
                  _resident(tab.shape)],
        out_specs=pl.BlockSpec((1, D_MODEL, 2 * GRID_W), lambda b, r: (b, 0, r // 2)),
        out_shape=jax.ShapeDtypeStruct((bsz, D_MODEL, n), BF16),
        compiler_params=_cparams(),
        name="na_lat_attention",
    )(q, k, kc, vt, vtc, tab)


def _na_ctx_kernel(q_ref, k_ref, v_ref, o_ref):
    for h in range(C_HEADS):
        hs = slice(h * C_DH, (h + 1) * C_DH)
        s = _dot_nt(q_ref[0, :, hs], k_ref[0, :, hs])
        p = jnp.exp(s - jnp.max(s, axis=-1, keepdims=True))
        o = _dot(p.astype(BF16), v_ref[0, :, hs]) / jnp.sum(p, axis=-1, keepdims=True)
        o_ref[0, :, hs] = o.astype(BF16)


def _na_ctx_attention(q, k, v):
    bsz, n, _ = q.shape
    spec = pl.BlockSpec((1, n, D_MODEL), lambda b: (b, 0, 0))
    return pl.pallas_call(
        _na_ctx_kernel,
        grid=(bsz,),
        in_specs=[spec, spec, spec],
        out_specs=spec,
        out_shape=jax.ShapeDtypeStruct((bsz, n, D_MODEL), BF16),
        compiler_params=_cparams(1),
        name="na_ctx_attention",
    )(q, k, v)


def kernel(x, c, ctx, c_ctx, w_ada, b_ada, norm_mix, norm_ffn, w_up, ffn_conv_w, ffn_conv_b, w_down,
           a_w_in, a_b_gate, a_conv_w, a_conv_b, a_w_out,
           b_w_in, b_w_a2, b_b_a, b_norm, b_w_out,
           c_w_in, c_rpb, c_w_out, norm_final):
    bsz = x.shape[0]
    depth = w_ada.shape[0]
    ctx_row = bsz
    assert bsz < COND_ROWS

    cond = jnp.concatenate([c, c_ctx[None], jnp.zeros((COND_ROWS - bsz - 1, D_MODEL), F32)], axis=0)
    mod_all = _ada_modulation(cond, w_ada, b_ada)
    x_ctx, x_lat = ctx, x

    for i in range(depth):
        last = i == depth - 1
        mods = mod_all[i].reshape(COND_ROWS, 1, 6 * D_MODEL)
        kind, j = i % 3, i // 3
        if kind == 0:
            acts_c, acts_l = _mlstm_mixer(x_ctx, x_lat, mods, ctx_row, norm_mix[i], a_w_in[j], a_b_gate[j],
                                          a_conv_w[j], a_conv_b[j])
            out = functools.partial(_mixer_out, _mlstm_out_kernel, "mlstm_out", consts=[], mods=mods,
                                    w_out=a_w_out[j], act_specs="ttt")
        elif kind == 1:
            acts_c, acts_l = _gla_mixer(x_ctx, x_lat, mods, ctx_row, norm_mix[i], b_w_in[j], b_w_a2[j], b_b_a[j])
            out = functools.partial(_mixer_out, _gla_out_kernel, "gla_out", consts=[b_norm[j].reshape(1, B_DV)],
                                    mods=mods, w_out=b_w_out[j], act_specs="rrr")
        else:
            qc, kc, vc, vtc = _na_proj(x_ctx, mods, ctx_row, norm_mix[i], c_w_in[j], with_v=True)
            ql, kl, vtl = _na_proj(x_lat, mods, None, norm_mix[i], c_w_in[j], with_v=False)
            ot = _na_lat_attention(ql, kl, kc, vtl, vtc, _na_bias_table(c_rpb[j]))
            mid_lat = _mixer_out(_na_out_t_kernel, "na_out_lat", x_lat, "t", [ot], [], mods, None, c_w_out[j])
            if not last:
                oc = _na_ctx_attention(qc, kc, vc)
                mid_ctx = _mixer_out(_na_out_kernel, "na_out_ctx", x_ctx, "r", [oc], [], mods, ctx_row, c_w_out[j])
        if kind != 2:
            mid_lat = out(x=x_lat, acts=list(acts_l), cond_row=None)
            if not last:
                mid_ctx = out(x=x_ctx, acts=list(acts_c), cond_row=ctx_row)
        ffn_w = _ffn_weights(w_up[i], ffn_conv_w[i], ffn_conv_b[i], w_down[i])
        x_lat = _conv_ffn(mid_lat, mods, None, norm_ffn[i], ffn_w, norm_final, final=last)
        if not last:
            x_ctx = _conv_ffn(mid_ctx, mods, ctx_row, norm_ffn[i], ffn_w, norm_final, final=False)
    return x_lat
```

```python
import functools

import jax
import jax.numpy as jnp
import numpy as np
from jax import lax
from jax.experimental import pallas as pl
from jax.experimental.pallas import tpu as pltpu

F32 = jnp.float32
BF16 = jnp.bfloat16

D_MODEL = 1024
EPS = 1e-6
GRID_W = 64

HALO = 16
VMEM_LIMIT = 56 * 1024 * 1024
COND_ROWS = 16

A_HEADS, A_DK, A_DV, A_CHUNK = 8, 64, 128, 128
A_QK, A_V = A_HEADS * A_DK, A_HEADS * A_DV
A_ONES = 16
A_CHUNKS_PER_STEP = 4
A_STATE = (A_HEADS, A_DV + A_ONES, A_DK)
A_MSTATE = (A_HEADS, 128)
B_HEADS, B_DK, B_DV, B_CHUNK = 4, 128, 256, 64
B_QK, B_V, B_RANK, B_TAU = B_HEADS * B_DK, B_HEADS * B_DV, 16, 16.0
B_SUB = 8
B_CHUNKS_PER_STEP = 4
C_HEADS, C_DH, C_WIN_ROWS, C_WIN_COLS = 16, 64, 8, 16
C_GROUP = 4
C_KROWS = C_WIN_ROWS + 2
FFN_DIM = 2816
FFN_TILE = 256
GATE_PAD = 128

NT_DIMS = (((1,), (1,)), ((), ()))
TN_DIMS = (((0,), (0,)), ((), ()))


def _cparams(n_axes=2):
    return pltpu.CompilerParams(dimension_semantics=("arbitrary",) * n_axes, vmem_limit_bytes=VMEM_LIMIT)


def _resident(shape):
    nd = len(shape)
    return pl.BlockSpec(shape, lambda *_: (0,) * nd, pipeline_mode=pl.Buffered(1))


def _block_rows(n_tokens):
    return 512 if n_tokens % 512 == 0 else 256


def _dot(a, b):
    return jnp.dot(a, b, preferred_element_type=F32)


def _dot_nt(a, b):
    return lax.dot_general(a, b, NT_DIMS, preferred_element_type=F32)


def _dot_tn(a, b):
    return lax.dot_general(a, b, TN_DIMS, preferred_element_type=F32)


def _split3(x):
    x1 = x.astype(BF16)
    r1 = x - x1.astype(F32)
    x2 = r1.astype(BF16)
    x3 = (r1 - x2.astype(F32)).astype(BF16)
    return x1, x2, x3


def _cumsum_rows(mask_bf16, x):
    x1, x2, x3 = _split3(x)
    return _dot(mask_bf16, x1) + _dot(mask_bf16, x2) + _dot(mask_bf16, x3)


def _cumsum_cols(x, mask_bf16):
    x1, x2, x3 = _split3(x)
    return _dot_nt(x1, mask_bf16) + _dot_nt(x2, mask_bf16) + _dot_nt(x3, mask_bf16)


def _log_sigmoid(z):
    return jnp.minimum(z, 0.0) - jnp.log1p(jnp.exp(-jnp.abs(z)))


def _silu(z):
    return z * jax.nn.sigmoid(z)


def _rms(x, gain):
    return x * lax.rsqrt(jnp.mean(x * x, axis=-1, keepdims=True) + EPS) * gain


def _norm_mod(x, gain, scale, shift):
    return _rms(x, gain) * (1.0 + scale) + shift


def _mod_part(mod, i):
    return mod[:, i * D_MODEL:(i + 1) * D_MODEL]


def _dwconv3(u, w, b, rows):
    n = rows + 2 * HALO
    prev = pltpu.roll(u, 1, 0)[HALO:HALO + rows]
    nxt = pltpu.roll(u, n - 1, 0)[HALO:HALO + rows]
    return prev * w[0:1] + u[HALO:HALO + rows] * w[1:2] + nxt * w[2:3] + b


def _fill_halo_block(hb_ref, xp_ref, xm_ref, xn_ref, gain, scale, shift, rows):
    t = pl.program_id(1)
    has_prev = (t != 0).astype(F32)
    has_next = (t != pl.num_programs(1) - 1).astype(F32)
    hb_ref[0:HALO] = (_norm_mod(xp_ref[0], gain, scale, shift) * has_prev).astype(BF16)
    hb_ref[HALO:HALO + rows] = _norm_mod(xm_ref[0], gain, scale, shift).astype(BF16)
    hb_ref[HALO + rows:] = (_norm_mod(xn_ref[0], gain, scale, shift) * has_next).astype(BF16)


def _halo_specs(n_tokens, rows):
    per = rows // HALO
    last = n_tokens // HALO - 1
    prev = pl.BlockSpec((1, HALO, D_MODEL), lambda b, t: (b, jnp.maximum(t * per - 1, 0), 0))
    main = pl.BlockSpec((1, rows, D_MODEL), lambda b, t: (b, t, 0))
    nxt = pl.BlockSpec((1, HALO, D_MODEL), lambda b, t: (b, jnp.minimum((t + 1) * per, last), 0))
    return prev, main, nxt


def _mod_spec(cond_row):
    if cond_row is None:
        return pl.BlockSpec((1, 1, 6 * D_MODEL), lambda b, t: (b, 0, 0))
    return pl.BlockSpec((1, 1, 6 * D_MODEL), lambda b, t: (cond_row, 0, 0))


def _row_spec(rows, width):
    return pl.BlockSpec((1, rows, width), lambda b, t: (b, t, 0))


def _ada_kernel(c_ref, w_ref, b_ref, o_ref):
    s = _silu(c_ref[...]).astype(BF16)
    o_ref[0] = _dot(s, w_ref[0].astype(BF16)) + b_ref[0]


def _ada_modulation(cond, w_ada, b_ada):
    depth, _, n = w_ada.shape
    tn = 1024
    return pl.pallas_call(
        _ada_kernel,
        grid=(depth, n // tn),
        in_specs=[pl.BlockSpec((COND_ROWS, D_MODEL), lambda l, j: (0, 0)),
                  pl.BlockSpec((1, D_MODEL, tn), lambda l, j: (l, 0, j)),
                  pl.BlockSpec((1, 1, tn), lambda l, j: (l, 0, j))],
        out_specs=pl.BlockSpec((1, COND_ROWS, tn), lambda l, j: (l, 0, j)),
        out_shape=jax.ShapeDtypeStruct((depth, COND_ROWS, n), F32),
        compiler_params=_cparams(),
        name="ada_modulation",
    )(cond, w_ada, b_ada.reshape(depth, 1, n))


def _ffn_kernel(xp_ref, xm_ref, xn_ref, mod_ref, gain_ref, wup_ref, cw_ref, cb_ref, wdn_ref, gfin_ref,
                out_ref, hb_ref, z_ref, *, rows, final):
    mod = mod_ref[0]
    shift, scale, gate = _mod_part(mod, 3), _mod_part(mod, 4), _mod_part(mod, 5)
    _fill_halo_block(hb_ref, xp_ref, xm_ref, xn_ref, gain_ref[...], scale, shift, rows)
    for c in range(FFN_DIM // FFN_TILE):
        ca = slice(c * FFN_TILE, (c + 1) * FFN_TILE)
        cg = slice(FFN_DIM + c * FFN_TILE, FFN_DIM + (c + 1) * FFN_TILE)
        a = _dwconv3(_dot(hb_ref[...], wup_ref[:, ca]), cw_ref[:, ca], cb_ref[:, ca], rows)
        g = _dwconv3(_dot(hb_ref[...], wup_ref[:, cg]), cw_ref[:, cg], cb_ref[:, cg], rows)
        z_ref[:, ca] = (_silu(g) * a).astype(BF16)
    y = xm_ref[0] + gate * _dot(z_ref[...], wdn_ref[...])
    if final:
        y = _rms(y, gfin_ref[...])
    out_ref[0] = y


def _ffn_weights(w_up, conv_w, conv_b, w_down):
    return w_up.astype(BF16), conv_w, conv_b.reshape(1, -1), w_down.astype(BF16)


def _conv_ffn(x, mods, cond_row, gain, weights, gain_final, *, final):
    bsz, n, _ = x.shape
    rows = _block_rows(n)
    wup, cw, cb, wdn = weights
    prev, main, nxt = _halo_specs(n, rows)
    return pl.pallas_call(
        functools.partial(_ffn_kernel, rows=rows, final=final),
        grid=(bsz, n // rows),
        in_specs=[prev, main, nxt, _mod_spec(cond_row), _resident((1, D_MODEL)),
                  _resident(wup.shape), _resident(cw.shape), _resident(cb.shape), _resident(wdn.shape),
                  _resident((1, D_MODEL))],
        out_specs=_row_spec(rows, D_MODEL),
        out_shape=jax.ShapeDtypeStruct((bsz, n, D_MODEL), F32),
        scratch_shapes=[pltpu.VMEM((rows + 2 * HALO, D_MODEL), BF16),
                        pltpu.VMEM((rows, FFN_DIM), BF16)],
        compiler_params=_cparams(),
        name="conv_ffn",
    )(x, x, x, mods, gain.reshape(1, D_MODEL), wup, cw, cb, wdn, gain_final.reshape(1, D_MODEL))


def _mlstm_out_kernel(x_ref, hft_ref, hbt_ref, ot_ref, mod_ref, w_ref, out_ref):
    h_sum = hft_ref[0].astype(F32) + hbt_ref[0].astype(F32)
    zt = (h_sum * jax.nn.sigmoid(ot_ref[0].astype(F32))).astype(BF16)
    out_ref[0] = x_ref[0] + _mod_part(mod_ref[0], 2) * _dot_tn(zt, w_ref[...])


def _gla_out_kernel(x_ref, of_ref, ob_ref, r_ref, gn_ref, mod_ref, w_ref, out_ref):
    oo = of_ref[0].astype(F32) + ob_ref[0].astype(F32)
    parts = [_rms(oo[:, h * B_DV:(h + 1) * B_DV], gn_ref[...]) for h in range(B_HEADS)]
    z = (jnp.concatenate(parts, axis=1) * _silu(r_ref[0].astype(F32))).astype(BF16)
    out_ref[0] = x_ref[0] + _mod_part(mod_ref[0], 2) * _dot(z, w_ref[...])


def _na_out_kernel(x_ref, o_ref, mod_ref, w_ref, out_ref):
    out_ref[0] = x_ref[0] + _mod_part(mod_ref[0], 2) * _dot(o_ref[0], w_ref[...])


def _na_out_t_kernel(x_ref, ot_ref, mod_ref, w_ref, out_ref):
    out_ref[0] = x_ref[0] + _mod_part(mod_ref[0], 2) * _dot_tn(ot_ref[0], w_ref[...])


def _mixer_out(kernel_fn, name, x, act_specs, acts, consts, mods, cond_row, w_out):
    bsz, n, _ = x.shape
    rows = _block_rows(n)
    w = w_out.astype(BF16)
    specs = [pl.BlockSpec((1, a.shape[1], rows), lambda b, t: (b, 0, t)) if s == "t"
             else _row_spec(rows, a.shape[-1]) for s, a in zip(act_specs, acts)]
    return pl.pallas_call(
        kernel_fn,
        grid=(bsz, n // rows),
        in_specs=([_row_spec(rows, D_MODEL)] + specs + [_resident(c.shape) for c in consts]
                  + [_mod_spec(cond_row), _resident(w.shape)]),
        out_specs=_row_spec(rows, D_MODEL),
        out_shape=jax.ShapeDtypeStruct((bsz, n, D_MODEL), F32),
        compiler_params=_cparams(),
        name=name,
    )(x, *acts, *consts, mods, w)


def _mlstm_proj_kernel(xp_ref, xm_ref, xn_ref, mod_ref, gain_ref, wqk_ref, cw_ref, cb_ref, wvot_ref, wg_ref,
                       bg_ref, wgt_ref, bgt_ref, q_ref, k_ref, vt_ref, ot_ref, bc_ref, gr_ref, bm_ref, hb_ref,
                       *, rows):
    mod = mod_ref[0]
    shift, scale = _mod_part(mod, 0), _mod_part(mod, 1)
    _fill_halo_block(hb_ref, xp_ref, xm_ref, xn_ref, gain_ref[...], scale, shift, rows)
    qk = _silu(_dwconv3(_dot(hb_ref[...], wqk_ref[...]), cw_ref[...], cb_ref[...], rows))
    q_ref[0] = (qk[:, :A_QK] * (A_DK ** -0.5)).astype(BF16)
    k_ref[0] = qk[:, A_QK:].astype(BF16)
    hm = hb_ref[HALO:HALO + rows]
    vot = _dot_nt(wvot_ref[...], hm)
    vt_ref[0] = vot[:A_V].astype(BF16)
    ot_ref[0] = vot[A_V:].astype(BF16)
    L, H = A_CHUNK, A_HEADS
    gates_c = _dot(hm, wg_ref[...]) + bg_ref[...]
    gates_r = _dot_nt(wgt_ref[...], hm) + bgt_ref[...]
    key = lax.broadcasted_iota(jnp.int32, (L, L), 0)
    qry = lax.broadcasted_iota(jnp.int32, (L, L), 1)
    fwd_mask = jnp.where(qry <= key, 1.0, 0.0).astype(BF16)
    bwd_mask = jnp.where(qry >= key, 1.0, 0.0).astype(BF16)
    fwd_lane = lax.broadcasted_iota(jnp.int32, (L, GATE_PAD), 1) < H
    fwd_row = lax.broadcasted_iota(jnp.int32, (2 * H, L), 0) < H
    lane = lax.broadcasted_iota(jnp.int32, (2 * H, L), 1)
    for c in range(rows // L):
        cs = slice(c * L, (c + 1) * L)
        logf_c = _log_sigmoid(gates_c[cs, GATE_PAD:])
        g_c = jnp.where(fwd_lane, _cumsum_rows(fwd_mask, logf_c), _cumsum_rows(bwd_mask, logf_c))
        bc_ref[0, cs, :] = gates_c[cs, :GATE_PAD] - g_c
        logf_r = _log_sigmoid(gates_r[2 * H:, cs])
        g_r = jnp.where(fwd_row, _cumsum_cols(logf_r, fwd_mask), _cumsum_cols(logf_r, bwd_mask))
        gr_ref[0, :, cs] = g_r
        run_max = gates_r[:2 * H, cs] - g_r
        dist = 1
        while dist < L:
            seen_f = jnp.where(lane >= dist, pltpu.roll(run_max, dist, 1), -jnp.inf)
            seen_b = jnp.where(lane < L - dist, pltpu.roll(run_max, L - dist, 1), -jnp.inf)
            run_max = jnp.maximum(run_max, jnp.where(fwd_row, seen_f, seen_b))
            dist *= 2
        bm_ref[0, :, cs] = run_max


def _mlstm_weights(w_in, b_gate, conv_w, conv_b):
    hh = A_HEADS
    wqk = w_in[:, :2 * A_QK].astype(BF16)
    wvot = w_in[:, 2 * A_QK:2 * A_QK + 2 * A_V].T.astype(BF16)
    wgate = w_in[:, 2 * A_QK + 2 * A_V:]
    pad = GATE_PAD - 2 * hh

    def regroup(t, axis, pad):
        i_f, f_f, i_b, f_b = jnp.split(t, 4, axis=axis)
        widths = [(0, 0)] * (t.ndim - 1) + [(0, pad)]
        return jnp.concatenate([jnp.pad(jnp.concatenate([i_f, i_b], axis), widths),
                                jnp.pad(jnp.concatenate([f_f, f_b], axis), widths)], axis)

    wg = regroup(wgate, 1, pad).astype(BF16)
    bg = regroup(b_gate, 0, pad).reshape(1, 2 * GATE_PAD)
    wgt = regroup(wgate, 1, 0).T.astype(BF16)
    bgt = regroup(b_gate, 0, 0).reshape(4 * hh, 1)
    return wqk, conv_w, conv_b.reshape(1, -1), wvot, wg, bg, wgt, bgt


def _mlstm_proj(x, mods, cond_row, gain, weights):
    bsz, n, _ = x.shape
    rows = _block_rows(n)
    prev, main, nxt = _halo_specs(n, rows)
    t_spec = pl.BlockSpec((1, A_V, rows), lambda b, t: (b, 0, t))
    g_spec = pl.BlockSpec((1, 2 * A_HEADS, rows), lambda b, t: (b, 0, t))
    g_shape = jax.ShapeDtypeStruct((bsz, 2 * A_HEADS, n), F32)
    return pl.pallas_call(
        functools.partial(_mlstm_proj_kernel, rows=rows),
        grid=(bsz, n // rows),
        in_specs=[prev, main, nxt, _mod_spec(cond_row), _resident((1, D_MODEL))]
                 + [_resident(w.shape) for w in weights],
        out_specs=[_row_spec(rows, A_QK), _row_spec(rows, A_QK), t_spec, t_spec,
                   _row_spec(rows, GATE_PAD), g_spec, g_spec],
        out_shape=[jax.ShapeDtypeStruct((bsz, n, A_QK), BF16), jax.ShapeDtypeStruct((bsz, n, A_QK), BF16),
                   jax.ShapeDtypeStruct((bsz, A_V, n), BF16), jax.ShapeDtypeStruct((bsz, A_V, n), BF16),
                   jax.ShapeDtypeStruct((bsz, n, GATE_PAD), F32), g_shape, g_shape],
        scratch_shapes=[pltpu.VMEM((rows + 2 * HALO, D_MODEL), BF16)],
        compiler_params=_cparams(),
        name="mlstm_proj",
    )(x, x, x, mods, gain.reshape(1, D_MODEL), *weights)


def _mlstm_scan_kernel(q_ref, k_ref, vt_ref, bc_ref, gr_ref, bm_ref, s0_ref, m0_ref,
                       ht_ref, s_out_ref, m_out_ref, state_ref, m_ref, *, reverse, cps):
    L, H = A_CHUNK, A_HEADS
    step = pl.program_id(1)

    @pl.when(step == 0)
    def _():
        state_ref[...] = s0_ref[0]
        m_ref[...] = m0_ref[0]

    key = lax.broadcasted_iota(jnp.int32, (L, L), 0)
    qry = lax.broadcasted_iota(jnp.int32, (L, L), 1)
    keep_t = (key >= qry) if reverse else (key <= qry)
    lane0 = H if reverse else 0
    end = 0 if reverse else L - 1

    order = range(cps - 1, -1, -1) if reverse else range(cps)
    span = lambda c: slice(c * L, (c + 1) * L)
    head_sub = lax.broadcasted_iota(jnp.int32, A_MSTATE, 0)
    head_lane = lax.broadcasted_iota(jnp.int32, A_MSTATE, 1)

    m_prev = m_ref[...]
    mx, inter, floor, decay, b_c, w_c = {}, {}, {}, {}, {}, {}
    for c in order:
        cum_r = gr_ref[0, lane0:lane0 + H, span(c)]
        run_max = bm_ref[0, lane0:lane0 + H, span(c)]
        mx[c] = jnp.maximum(run_max, m_prev)
        mx_end = jnp.broadcast_to(mx[c][:, end:end + 1], A_MSTATE)
        g_tot = jnp.broadcast_to(cum_r[:, end:end + 1], A_MSTATE)
        inter[c] = jnp.exp(m_prev - mx[c])
        floor[c] = jnp.exp(-(cum_r + mx[c]))
        decay[c] = jnp.exp(m_prev - mx_end)
        mx_end_lanes = jnp.sum(jnp.where(head_lane == head_sub + lane0, mx_end, 0.0), axis=0, keepdims=True)
        b_c[c] = bc_ref[0, span(c), :]
        w_c[c] = jnp.exp(b_c[c] - mx_end_lanes)
        m_prev = g_tot + mx_end
    m_ref[...] = m_prev

    items = [(c, h) for c in range(cps) for h in range(H)]
    ones = jnp.ones((A_ONES, L), BF16)
    qs = {(c, h): q_ref[0, span(c), h * A_DK:(h + 1) * A_DK] for c, h in items}
    ks = {(c, h): k_ref[0, span(c), h * A_DK:(h + 1) * A_DK] for c, h in items}
    vaug_t = {(c, h): jnp.concatenate([vt_ref[0, h * A_DV:(h + 1) * A_DV, span(c)], ones], axis=0)
              for c, h in items}
    s_t = {i: _dot_nt(ks[i], qs[i]) for i in items}
    state_add = {(c, h): _dot(vaug_t[c, h],
                              (ks[c, h].astype(F32) * w_c[c][:, lane0 + h:lane0 + h + 1]).astype(BF16))
                 for c, h in items}
    p_t = {}
    for c, h in items:
        b_col = b_c[c][:, lane0 + h:lane0 + h + 1]
        w_t = jnp.exp(jnp.where(keep_t, b_col - mx[c][h:h + 1, :], -jnp.inf))
        p_t[c, h] = (s_t[c, h] * w_t).astype(BF16)
    local_t = {i: _dot(vaug_t[i], p_t[i]) for i in items}

    for h in range(H):
        state = state_ref[h]
        for c in order:
            tot = local_t[c, h] + inter[c][h:h + 1, :] * _dot_nt(state.astype(BF16), qs[c, h])
            den = jnp.maximum(jnp.abs(tot[A_DV:A_DV + 1, :]), floor[c][h:h + 1, :])
            ht_ref[0, h * A_DV:(h + 1) * A_DV, span(c)] = (tot[:A_DV] / den).astype(BF16)
            state = decay[c][h:h + 1, 0:1] * state + state_add[c, h]
        state_ref[h] = state

    @pl.when(step == pl.num_programs(1) - 1)
    def _():
        s_out_ref[0] = state_ref[...]
        m_out_ref[0] = m_ref[...]


def _chunk_order(n_chunks, reverse):
    return (lambda c: n_chunks - 1 - c) if reverse else (lambda c: c)


def _mlstm_scan(q, k, vt, b_c, g_r, bmax_r, state0, m0, *, reverse):
    bsz, n, _ = q.shape
    cps = min(A_CHUNKS_PER_STEP, n // A_CHUNK)
    L = A_CHUNK * cps
    cidx = _chunk_order(n // L, reverse)
    tok = lambda w: pl.BlockSpec((1, L, w), lambda b, c: (b, cidx(c), 0))
    chan = lambda r: pl.BlockSpec((1, r, L), lambda b, c: (b, 0, cidx(c)))
    s_spec = pl.BlockSpec((1,) + A_STATE, lambda b, c: (b, 0, 0, 0))
    m_spec = pl.BlockSpec((1,) + A_MSTATE, lambda b, c: (b, 0, 0))
    return pl.pallas_call(
        functools.partial(_mlstm_scan_kernel, reverse=reverse, cps=cps),
        grid=(bsz, n // L),
        in_specs=[tok(A_QK), tok(A_QK), chan(A_V), tok(GATE_PAD),
                  chan(2 * A_HEADS), chan(2 * A_HEADS), s_spec, m_spec],
        out_specs=[chan(A_V), s_spec, m_spec],
        out_shape=[jax.ShapeDtypeStruct((bsz, A_V, n), BF16),
                   jax.ShapeDtypeStruct((bsz,) + A_STATE, F32),
                   jax.ShapeDtypeStruct((bsz,) + A_MSTATE, F32)],
        scratch_shapes=[pltpu.VMEM(A_STATE, F32), pltpu.VMEM(A_MSTATE, F32)],
        compiler_params=_cparams(),
        name="mlstm_scan_bwd" if reverse else "mlstm_scan_fwd",
    )(q, k, vt, b_c, g_r, bmax_r, state0, m0)


def _mlstm_mixer(x_ctx, x_lat, mods, ctx_row, gain, w_in, b_gate, conv_w, conv_b):
    weights = _mlstm_weights(w_in, b_gate, conv_w, conv_b)
    bsz = x_lat.shape[0]
    streams = []
    for x, row in ((x_ctx, ctx_row), (x_lat, None)):
        q, k, vt, ot, b_c, g_r, bmax_r = _mlstm_proj(x, mods, row, gain, weights)
        streams.append(((q, k, vt, b_c, g_r, bmax_r), ot))
    outs = []
    for reverse in (False, True):
        state = jnp.zeros((bsz,) + A_STATE, F32)
        m = jnp.zeros((bsz,) + A_MSTATE, F32)
        res = []
        for args, _ in streams:
            ht, state, m = _mlstm_scan(*args, state, m, reverse=reverse)
            res.append(ht)
        outs.append(res)
    (hcf, hlf), (hcb, hlb) = outs
    return (hcf, hcb, streams[0][1]), (hlf, hlb, streams[1][1])


def _gla_proj_kernel(x_ref, mod_ref, gain_ref, wqkv_ref, wr_ref, wa_ref, wa2_ref, ba_ref,
                     q_ref, k_ref, v_ref, r_ref, gf_ref, gb_ref, *, rows):
    mod = mod_ref[0]
    h = _norm_mod(x_ref[0], gain_ref[...], _mod_part(mod, 1), _mod_part(mod, 0)).astype(BF16)
    qkv = _dot(h, wqkv_ref[...])
    q_ref[0] = qkv[:, :B_QK].astype(BF16)
    k_ref[0] = qkv[:, B_QK:2 * B_QK].astype(BF16)
    v_ref[0] = qkv[:, 2 * B_QK:].astype(BF16)
    r_ref[0] = _dot(h, wr_ref[...]).astype(BF16)
    L = B_CHUNK
    a = _dot(h, wa_ref[...]).astype(BF16)
    row = lax.broadcasted_iota(jnp.int32, (L, L), 0)
    col = lax.broadcasted_iota(jnp.int32, (L, L), 1)
    for d, out_ref in enumerate((gf_ref, gb_ref)):
        mask = jnp.where((col >= row) if d else (col <= row), 1.0, 0.0).astype(BF16)
        g = _log_sigmoid(_dot(a, wa2_ref[d]) + ba_ref[d]) / B_TAU
        for c in range(rows // L):
            out_ref[0, c * L:(c + 1) * L, :] = _cumsum_rows(mask, g[c * L:(c + 1) * L])


def _gla_proj(x, mods, cond_row, gain, weights):
    bsz, n, _ = x.shape
    rows = _block_rows(n)
    return pl.pallas_call(
        functools.partial(_gla_proj_kernel, rows=rows),
        grid=(bsz, n // rows),
        in_specs=[_row_spec(rows, D_MODEL), _mod_spec(cond_row), _resident((1, D_MODEL))]
                 + [_resident(w.shape) for w in weights],
        out_specs=[_row_spec(rows, B_QK), _row_spec(rows, B_QK), _row_spec(rows, B_V), _row_spec(rows, B_V),
                   _row_spec(rows, B_QK), _row_spec(rows, B_QK)],
        out_shape=[jax.ShapeDtypeStruct((bsz, n, B_QK), BF16), jax.ShapeDtypeStruct((bsz, n, B_QK), BF16),
                   jax.ShapeDtypeStruct((bsz, n, B_V), BF16), jax.ShapeDtypeStruct((bsz, n, B_V), BF16),
                   jax.ShapeDtypeStruct((bsz, n, B_QK), F32), jax.ShapeDtypeStruct((bsz, n, B_QK), F32)],
        compiler_params=_cparams(),
        name="gla_proj",
    )(x, mods, gain.reshape(1, D_MODEL), *weights)


def _gla_scan_kernel(q_ref, k_ref, v_ref, g_ref, s0_ref, o_ref, s_out_ref, state_ref, *, reverse, cps):
    L, sub = B_CHUNK, B_SUB
    nsub = L // sub
    step = pl.program_id(1)

    @pl.when(step == 0)
    def _():
        state_ref[...] = s0_ref[0]

    pair = lax.broadcasted_iota(jnp.int32, (L * sub, 128), 0)
    lane = lax.broadcasted_iota(jnp.int32, (L * sub, 128), 1)
    pair_q, pair_s = pair // sub, pair % sub
    place = lane == (pair_q // sub) * sub + pair_s
    sel = (lax.broadcasted_iota(jnp.int32, (L, L * sub), 1) // sub
           == lax.broadcasted_iota(jnp.int32, (L, L * sub), 0))
    sel_bf16 = jnp.where(sel, 1.0, 0.0).astype(BF16)
    ones_red = jnp.ones((B_DK, 128), BF16)
    srow = lax.broadcasted_iota(jnp.int32, (sub, B_DK), 0)

    items = [(c, h) for c in range(cps) for h in range(B_HEADS)]
    rows = lambda c: slice(c * L, (c + 1) * L)
    Gs = {(c, h): g_ref[0, rows(c), h * B_DK:(h + 1) * B_DK] for c, h in items}
    qs = {(c, h): q_ref[0, rows(c), h * B_DK:(h + 1) * B_DK].astype(F32) * (B_DK ** -0.5) for c, h in items}
    ks = {(c, h): k_ref[0, rows(c), h * B_DK:(h + 1) * B_DK].astype(F32) for c, h in items}
    vs = {(c, h): v_ref[0, rows(c), h * B_DV:(h + 1) * B_DV] for c, h in items}
    g_tot = {i: (G[0:1] if reverse else G[L - 1:L]) for i, G in Gs.items()}
    q_carry = {i: (qs[i] * jnp.exp(Gs[i])).astype(BF16) for i in items}
    state_add = {i: _dot_tn(vs[i], (ks[i] * jnp.exp(g_tot[i] - Gs[i])).astype(BF16)) for i in items}

    def cross_scores(G, qh, kh):
        def edge(b):
            r = b * sub if reverse else b * sub + sub - 1
            return G[r:r + 1]
        edges = jnp.concatenate([jnp.broadcast_to(edge(b), (sub, B_DK)) for b in range(nsub)], axis=0)
        k_edge = kh * jnp.exp(edges - G)
        q_cols, k_cols = [], []
        for b in range(nsub):
            lo, hi = b * sub, (b + 1) * sub
            before, after = jnp.zeros((lo, B_DK), F32), jnp.zeros((L - hi, B_DK), F32)
            if reverse:
                seen = [qh[:lo] * jnp.exp(G[:lo] - edge(b)), jnp.zeros((L - lo, B_DK), F32)]
            else:
                seen = [jnp.zeros((hi, B_DK), F32), qh[hi:] * jnp.exp(G[hi:] - edge(b))]
            q_cols.append(jnp.concatenate([s for s in seen if s.shape[0]], axis=0).astype(BF16))
            k_parts = [before, k_edge[lo:hi], after]
            k_cols.append(jnp.concatenate([s for s in k_parts if s.shape[0]], axis=0).astype(BF16))
        return _dot_nt(jnp.concatenate(q_cols, axis=1), jnp.concatenate(k_cols, axis=1))

    att_cross = {i: cross_scores(Gs[i], qs[i], ks[i]) for i in items}

    def pair_sums(G, qh, kh):
        pieces = []
        for b in range(nsub):
            Gb = G[b * sub:(b + 1) * sub]
            kb = kh[b * sub:(b + 1) * sub]
            for j in range(sub):
                r = b * sub + j
                vis = (srow >= j) if reverse else (srow <= j)
                pieces.append(qh[r:r + 1] * kb * jnp.exp(jnp.where(vis, G[r:r + 1] - Gb, -jnp.inf)))
        pair_terms = jnp.concatenate(pieces, axis=0).astype(BF16)
        return _dot(pair_terms, ones_red)

    sums = {i: pair_sums(Gs[i], qs[i], ks[i]) for i in items}
    att_local = {i: _dot(sel_bf16, jnp.where(place, sums[i], 0.0).astype(BF16)) for i in items}
    o_local = {i: _dot((att_cross[i] + att_local[i][:, :L]).astype(BF16), vs[i]) for i in items}

    order = range(cps - 1, -1, -1) if reverse else range(cps)
    for h in range(B_HEADS):
        state = state_ref[h]
        for c in order:
            o_ref[0, rows(c), h * B_DV:(h + 1) * B_DV] = (
                o_local[c, h] + _dot_nt(q_carry[c, h], state.astype(BF16))).astype(BF16)
            state = state * jnp.exp(g_tot[c, h]) + state_add[c, h]
        state_ref[h] = state

    @pl.when(step == pl.num_programs(1) - 1)
    def _():
        s_out_ref[0] = state_ref[...]


def _gla_scan(q, k, v, g, state0, *, reverse):
    bsz, n, _ = q.shape
    L = B_CHUNK * B_CHUNKS_PER_STEP
    cidx = _chunk_order(n // L, reverse)
    sshape = (B_HEADS, B_DV, B_DK)
    return pl.pallas_call(
        functools.partial(_gla_scan_kernel, reverse=reverse, cps=B_CHUNKS_PER_STEP),
        grid=(bsz, n // L),
        in_specs=[pl.BlockSpec((1, L, B_QK), lambda b, c: (b, cidx(c), 0)),
                  pl.BlockSpec((1, L, B_QK), lambda b, c: (b, cidx(c), 0)),
                  pl.BlockSpec((1, L, B_V), lambda b, c: (b, cidx(c), 0)),
                  pl.BlockSpec((1, L, B_QK), lambda b, c: (b, cidx(c), 0)),
                  pl.BlockSpec((1,) + sshape, lambda b, c: (b, 0, 0, 0))],
        out_specs=[pl.BlockSpec((1, L, B_V), lambda b, c: (b, cidx(c), 0)),
                   pl.BlockSpec((1,) + sshape, lambda b, c: (b, 0, 0, 0))],
        out_shape=[jax.ShapeDtypeStruct((bsz, n, B_V), BF16), jax.ShapeDtypeStruct((bsz,) + sshape, F32)],
        scratch_shapes=[pltpu.VMEM(sshape, F32)],
        compiler_params=_cparams(),
        name="gla_scan_bwd" if reverse else "gla_scan_fwd",
    )(q, k, v, g, state0)


def _gla_mixer(x_ctx, x_lat, mods, ctx_row, gain, w_in, w_a2, b_a):
    wqkv = w_in[:, :2 * B_QK + B_V].astype(BF16)
    wr = w_in[:, 2 * B_QK + B_V:2 * B_QK + 2 * B_V].astype(BF16)
    wa = jnp.pad(w_in[:, 2 * B_QK + 2 * B_V:], ((0, 0), (0, GATE_PAD - 2 * B_RANK))).astype(BF16)
    wa2 = jnp.stack([jnp.zeros((GATE_PAD, B_QK), F32).at[d * B_RANK:(d + 1) * B_RANK].set(w_a2[d])
                     for d in range(2)]).astype(BF16)
    ba = b_a.reshape(2, 1, B_QK)
    bsz = x_lat.shape[0]
    pc = _gla_proj(x_ctx, mods, ctx_row, gain, (wqkv, wr, wa, wa2, ba))
    plat = _gla_proj(x_lat, mods, None, gain, (wqkv, wr, wa, wa2, ba))
    outs = []
    for d, reverse in enumerate((False, True)):
        state = jnp.zeros((bsz, B_HEADS, B_DV, B_DK), F32)
        res = []
        for p in (pc, plat):
            o, state = _gla_scan(p[0], p[1], p[2], p[4 + d], state, reverse=reverse)
            res.append(o)
        outs.append(res)
    (ocf, olf), (ocb, olb) = outs
    return (ocf, ocb, pc[3]), (olf, olb, plat[3])


def _na_proj_kernel_ctx(x_ref, mod_ref, gain_ref, w_ref, wvt_ref, q_ref, k_ref, v_ref, vt_ref):
    mod = mod_ref[0]
    h = _norm_mod(x_ref[0], gain_ref[...], _mod_part(mod, 1), _mod_part(mod, 0)).astype(BF16)
    qkv = _dot(h, w_ref[...])
    q_ref[0] = (qkv[:, :D_MODEL] * (C_DH ** -0.5)).astype(BF16)
    k_ref[0] = qkv[:, D_MODEL:2 * D_MODEL].astype(BF16)
    v_ref[0] = qkv[:, 2 * D_MODEL:].astype(BF16)
    vt_ref[0] = _dot_nt(wvt_ref[...], h).astype(BF16)


def _na_proj_kernel_lat(x_ref, mod_ref, gain_ref, w_ref, wvt_ref, q_ref, k_ref, vt_ref):
    mod = mod_ref[0]
    h = _norm_mod(x_ref[0], gain_ref[...], _mod_part(mod, 1), _mod_part(mod, 0)).astype(BF16)
    qk = _dot(h, w_ref[...])
    q_ref[0] = (qk[:, :D_MODEL] * (C_DH ** -0.5)).astype(BF16)
    k_ref[0] = qk[:, D_MODEL:].astype(BF16)
    vt_ref[0] = _dot_nt(wvt_ref[...], h).astype(BF16)


def _na_proj(x, mods, cond_row, gain, w_in, *, with_v):
    bsz, n, _ = x.shape
    rows = _block_rows(n)
    wvt = w_in[:, 2 * D_MODEL:].T.astype(BF16)
    w = (w_in if with_v else w_in[:, :2 * D_MODEL]).astype(BF16)
    tok = jax.ShapeDtypeStruct((bsz, n, D_MODEL), BF16)
    vt_shape = jax.ShapeDtypeStruct((bsz, D_MODEL, n), BF16)
    vt_spec = pl.BlockSpec((1, D_MODEL, rows), lambda b, t: (b, 0, t))
    rs = _row_spec(rows, D_MODEL)
    return pl.pallas_call(
        _na_proj_kernel_ctx if with_v else _na_proj_kernel_lat,
        grid=(bsz, n // rows),
        in_specs=[rs, _mod_spec(cond_row), _resident((1, D_MODEL)), _resident(w.shape), _resident(wvt.shape)],
        out_specs=[rs, rs, rs, vt_spec] if with_v else [rs, rs, vt_spec],
        out_shape=[tok, tok, tok, vt_shape] if with_v else [tok, tok, vt_shape],
        compiler_params=_cparams(),
        name="na_proj_ctx" if with_v else "na_proj_lat",
    )(x, mods, gain.reshape(1, D_MODEL), w, wvt)


def _na_bias_table(rpb):
    colq = np.arange(GRID_W)[None, :]
    colk = np.arange(GRID_W)[:, None]
    start = np.clip(colq - C_WIN_COLS // 2, 0, GRID_W - C_WIN_COLS)
    valid = (colk >= start) & (colk < start + C_WIN_COLS)
    dc = np.clip(colk - colq + C_WIN_COLS - 1, 0, 2 * C_WIN_COLS - 2)
    bias = jnp.where(valid[None, None], rpb[:, :, dc], -jnp.inf)
    bias = bias.transpose(1, 2, 0, 3).reshape(2 * C_WIN_ROWS - 1, GRID_W, C_HEADS * GRID_W)
    return jnp.concatenate([jnp.full((1,) + bias.shape[1:], -jnp.inf, F32), bias], axis=0)


def _na_lat_kernel(q_ref, k_ref, kc_ref, vt_ref, vtc_ref, tab_ref, ot_ref, *, rows):
    r0 = 2 * pl.program_id(1)
    gw = C_GROUP * C_DH
    nk = C_KROWS * GRID_W
    first = lambda r: jnp.clip(r - C_WIN_ROWS // 2, 0, rows - C_WIN_ROWS)
    rs_load = jnp.minimum(first(r0) - first(r0) % 2, rows - C_KROWS)
    start = pl.multiple_of(rs_load * GRID_W, 2 * GRID_W)
    entries = []
    for i in range(C_KROWS):
        kr = rs_load + i
        per_q = []
        for r in (r0, r0 + 1):
            ok = jnp.logical_and(kr >= first(r), kr < first(r) + C_WIN_ROWS)
            per_q.append(jnp.where(ok, kr - r + C_WIN_ROWS, 0))
        entries.append(per_q)

    blk_r = lax.broadcasted_iota(jnp.int32, (gw, gw), 0) // C_DH
    blk_c = lax.broadcasted_iota(jnp.int32, (gw, gw), 1) // C_DH
    same_head = blk_r == blk_c

    def block_diag(qg):
        return jnp.where(same_head, jnp.concatenate([qg] * C_GROUP, axis=0), 0).astype(BF16)

    groups = range(C_HEADS // C_GROUP)
    gsl = [slice(g * gw, (g + 1) * gw) for g in groups]
    q_bd = [jnp.concatenate([block_diag(q_ref[0, :GRID_W, gs]), block_diag(q_ref[0, GRID_W:, gs])], axis=0)
            for gs in gsl]
    s_loc = [_dot_nt(k_ref[0, pl.ds(start, nk), gs], qb) for gs, qb in zip(gsl, q_bd)]
    s_ctx = [_dot_nt(kc_ref[0, :, gs], qb) for gs, qb in zip(gsl, q_bd)]
    p_loc, p_ctx, denom = [], [], []
    for g in groups:
        bias = jnp.concatenate(
            [jnp.concatenate([tab_ref[e, :, gsl[g]] for e in per_q], axis=1) for per_q in entries], axis=0)
        sl = s_loc[g] + bias
        m = jnp.maximum(jnp.max(sl, axis=0, keepdims=True), jnp.max(s_ctx[g], axis=0, keepdims=True))
        pl_, pc_ = jnp.exp(sl - m), jnp.exp(s_ctx[g] - m)
        denom.append(jnp.sum(pl_, axis=0, keepdims=True) + jnp.sum(pc_, axis=0, keepdims=True))
        p_loc.append(pl_.astype(BF16))
        p_ctx.append(pc_.astype(BF16))
    for g in groups:
        ot = (_dot(vt_ref[0, gsl[g], pl.ds(start, nk)], p_loc[g])
              + _dot(vtc_ref[0, gsl[g], :], p_ctx[g])) / denom[g]
        for qr in range(2):
            for h in range(C_GROUP):
                ch = slice(g * gw + h * C_DH, g * gw + (h + 1) * C_DH)
                ot_ref[0, ch, qr * GRID_W:(qr + 1) * GRID_W] = (
                    ot[h * C_DH:(h + 1) * C_DH, qr * gw + h * C_DH:qr * gw + (h + 1) * C_DH].astype(BF16))


def _na_lat_attention(q, k, kc, vt, vtc, tab):
    bsz, n, _ = q.shape
    rows = n // GRID_W
    n_ctx = kc.shape[1]
    assert rows >= C_KROWS and rows % 2 == 0
    return pl.pallas_call(
        functools.partial(_na_lat_kernel, rows=rows),
        grid=(bsz, rows // 2),
        in_specs=[pl.BlockSpec((1, 2 * GRID_W, D_MODEL), lambda b, r: (b, r, 0)),
                  pl.BlockSpec((1, n, D_MODEL), lambda b, r: (b, 0, 0)),
                  pl.BlockSpec((1, n_ctx, D_MODEL), lambda b, r: (b, 0, 0)),
                  pl.BlockSpec((1, D_MODEL, n), lambda b, r: (b, 0, 0)),
                  pl.BlockSpec((1, D_MODEL, n_ctx), lambda b, r: (b, 0, 0)),
                  _resident(tab.shape)],
        out_specs=pl.BlockSpec((1, D_MODEL, 2 * GRID_W), lambda b, r: (b, 0, r)),
        out_shape=jax.ShapeDtypeStruct((bsz, D_MODEL, n), BF16),
        compiler_params=_cparams(),
        name="na_lat_attention",
    )(q, k, kc, vt, vtc, tab)


def _na_ctx_kernel(q_ref, k_ref, v_ref, o_ref):
    for h in range(C_HEADS):
        hs = slice(h * C_DH, (h + 1) * C_DH)
        s = _dot_nt(q_ref[0, :, hs], k_ref[0, :, hs])
        p = jnp.exp(s - jnp.max(s, axis=-1, keepdims=True))
        o = _dot(p.astype(BF16), v_ref[0, :, hs]) / jnp.sum(p, axis=-1, keepdims=True)
        o_ref[0, :, hs] = o.astype(BF16)


def _na_ctx_attention(q, k, v):
    bsz, n, _ = q.shape
    spec = pl.BlockSpec((1, n, D_MODEL), lambda b: (b, 0, 0))
    return pl.pallas_call(
        _na_ctx_kernel,
        grid=(bsz,),
        in_specs=[spec, spec, spec],
        out_specs=spec,
        out_shape=jax.ShapeDtypeStruct((bsz, n, D_MODEL), BF16),
        compiler_params=_cparams(1),
        name="na_ctx_attention",
    )(q, k, v)


def kernel(x, c, ctx, c_ctx, w_ada, b_ada, norm_mix, norm_ffn, w_up, ffn_conv_w, ffn_conv_b, w_down,
           a_w_in, a_b_gate, a_conv_w, a_conv_b, a_w_out,
           b_w_in, b_w_a2, b_b_a, b_norm, b_w_out,
           c_w_in, c_rpb, c_w_out, norm_final):
    bsz = x.shape[0]
    depth = w_ada.shape[0]
    ctx_row = bsz
    assert bsz < COND_ROWS

    cond = jnp.concatenate([c, c_ctx[None], jnp.zeros((COND_ROWS - bsz - 1, D_MODEL), F32)], axis=0)
    mod_all = _ada_modulation(cond, w_ada, b_ada)
    x_ctx, x_lat = ctx, x

    for i in range(depth):
        last = i == depth - 1
        mods = mod_all[i].reshape(COND_ROWS, 1, 6 * D_MODEL)
        kind, j = i % 3, i // 3
        if kind == 0:
            acts_c, acts_l = _mlstm_mixer(x_ctx, x_lat, mods, ctx_row, norm_mix[i], a_w_in[j], a_b_gate[j],
                                          a_conv_w[j], a_conv_b[j])
            out = functools.partial(_mixer_out, _mlstm_out_kernel, "mlstm_out", consts=[], mods=mods,
                                    w_out=a_w_out[j], act_specs="ttt")
        elif kind == 1:
            acts_c, acts_l = _gla_mixer(x_ctx, x_lat, mods, ctx_row, norm_mix[i], b_w_in[j], b_w_a2[j], b_b_a[j])
            out = functools.partial(_mixer_out, _gla_out_kernel, "gla_out", consts=[b_norm[j].reshape(1, B_DV)],
                                    mods=mods, w_out=b_w_out[j], act_specs="rrr")
        else:
            qc, kc, vc, vtc = _na_proj(x_ctx, mods, ctx_row, norm_mix[i], c_w_in[j], with_v=True)
            ql, kl, vtl = _na_proj(x_lat, mods, None, norm_mix[i], c_w_in[j], with_v=False)
            ot = _na_lat_attention(ql, kl, kc, vtl, vtc, _na_bias_table(c_rpb[j]))
            mid_lat = _mixer_out(_na_out_t_kernel, "na_out_lat", x_lat, "t", [ot], [], mods, None, c_w_out[j])
            if not last:
                oc = _na_ctx_attention(qc, kc, vc)
                mid_ctx = _mixer_out(_na_out_kernel, "na_out_ctx", x_ctx, "r", [oc], [], mods, ctx_row, c_w_out[j])
        if kind != 2:
            mid_lat = out(x=x_lat, acts=list(acts_l), cond_row=None)
            if not last:
                mid_ctx = out(x=x_ctx, acts=list(acts_c), cond_row=ctx_row)
        ffn_w = _ffn_weights(w_up[i], ffn_conv_w[i], ffn_conv_b[i], w_down[i])
        x_lat = _conv_ffn(mid_lat, mods, None, norm_ffn[i], ffn_w, norm_final, final=last)
        if not last:
            x_ctx = _conv_ffn(mid_ctx, mods, ctx_row, norm_ffn[i], ffn_w, norm_final, final=False)
    return x_lat
```

```python
import functools

import jax
import jax.numpy as jnp
import numpy as np
from jax import lax
from jax.experimental import pallas as pl
from jax.experimental.pallas import tpu as pltpu

F32 = jnp.float32
BF16 = jnp.bfloat16

D_MODEL = 1024
EPS = 1e-6
GRID_W = 64

HALO = 16
VMEM_LIMIT = 56 * 1024 * 1024
COND_ROWS = 16

A_HEADS, A_DK, A_DV, A_CHUNK = 8, 64, 128, 128
A_QK, A_V = A_HEADS * A_DK, A_HEADS * A_DV
A_ONES = 16
A_CHUNKS_PER_STEP = 4
A_STATE = (A_HEADS // 2, A_DV + A_ONES, 2 * A_DK)
A_MSTATE = (A_HEADS, 128)
B_HEADS, B_DK, B_DV, B_CHUNK = 4, 128, 256, 64
B_QK, B_V, B_RANK, B_TAU = B_HEADS * B_DK, B_HEADS * B_DV, 16, 16.0
B_SUB = 8
B_CHUNKS_PER_STEP = 4
C_HEADS, C_DH, C_WIN_ROWS, C_WIN_COLS = 16, 64, 8, 16
C_GROUP = 4
C_KROWS = C_WIN_ROWS + 2
FFN_DIM = 2816
FFN_TILE = 256
FFN_ROWS = 1024
GATE_PAD = 128

NT_DIMS = (((1,), (1,)), ((), ()))
TN_DIMS = (((0,), (0,)), ((), ()))


def _cparams(n_axes=2):
    return pltpu.CompilerParams(dimension_semantics=("arbitrary",) * n_axes, vmem_limit_bytes=VMEM_LIMIT)


def _resident(shape):
    nd = len(shape)
    return pl.BlockSpec(shape, lambda *_: (0,) * nd, pipeline_mode=pl.Buffered(1))


def _block_rows(n_tokens, largest=512):
    return max(r for r in (256, 512, 1024) if r <= largest and (n_tokens % r == 0 or r == 256))


def _dot(a, b):
    return jnp.dot(a, b, preferred_element_type=F32)


def _dot_nt(a, b):
    return lax.dot_general(a, b, NT_DIMS, preferred_element_type=F32)


def _dot_tn(a, b):
    return lax.dot_general(a, b, TN_DIMS, preferred_element_type=F32)


def _split3(x):
    x1 = x.astype(BF16)
    r1 = x - x1.astype(F32)
    x2 = r1.astype(BF16)
    x3 = (r1 - x2.astype(F32)).astype(BF16)
    return x1, x2, x3


def _cumsum_rows(mask_bf16, x):
    x1, x2, x3 = _split3(x)
    return _dot(mask_bf16, x1) + _dot(mask_bf16, x2) + _dot(mask_bf16, x3)


def _cumsum_cols(x, mask_bf16):
    x1, x2, x3 = _split3(x)
    return _dot_nt(x1, mask_bf16) + _dot_nt(x2, mask_bf16) + _dot_nt(x3, mask_bf16)


def _log_sigmoid(z):
    return jnp.minimum(z, 0.0) - jnp.log1p(jnp.exp(-jnp.abs(z)))


def _silu(z):
    return z * jax.nn.sigmoid(z)


def _rms(x, gain):
    return x * lax.rsqrt(jnp.mean(x * x, axis=-1, keepdims=True) + EPS) * gain


def _norm_mod(x, gain, scale, shift):
    return _rms(x, gain) * (1.0 + scale) + shift


def _mod_part(mod, i):
    return mod[:, i * D_MODEL:(i + 1) * D_MODEL]


def _dwconv3(u, w, b, rows):
    n = rows + 2 * HALO
    prev = pltpu.roll(u, 1, 0)[HALO:HALO + rows]
    nxt = pltpu.roll(u, n - 1, 0)[HALO:HALO + rows]
    return prev * w[0:1] + u[HALO:HALO + rows] * w[1:2] + nxt * w[2:3] + b


def _fill_halo_block(hb_ref, xp_ref, xm_ref, xn_ref, gain, scale, shift, rows):
    t = pl.program_id(1)
    has_prev = (t != 0).astype(F32)
    has_next = (t != pl.num_programs(1) - 1).astype(F32)
    hb_ref[0:HALO] = (_norm_mod(xp_ref[0], gain, scale, shift) * has_prev).astype(BF16)
    hb_ref[HALO:HALO + rows] = _norm_mod(xm_ref[0], gain, scale, shift).astype(BF16)
    hb_ref[HALO + rows:] = (_norm_mod(xn_ref[0], gain, scale, shift) * has_next).astype(BF16)


def _halo_specs(n_tokens, rows):
    per = rows // HALO
    last = n_tokens // HALO - 1
    prev = pl.BlockSpec((1, HALO, D_MODEL), lambda b, t: (b, jnp.maximum(t * per - 1, 0), 0))
    main = pl.BlockSpec((1, rows, D_MODEL), lambda b, t: (b, t, 0))
    nxt = pl.BlockSpec((1, HALO, D_MODEL), lambda b, t: (b, jnp.minimum((t + 1) * per, last), 0))
    return prev, main, nxt


def _mod_spec(cond_row):
    if cond_row is None:
        return pl.BlockSpec((1, 1, 6 * D_MODEL), lambda b, t: (b, 0, 0))
    return pl.BlockSpec((1, 1, 6 * D_MODEL), lambda b, t: (cond_row, 0, 0))


def _row_spec(rows, width):
    return pl.BlockSpec((1, rows, width), lambda b, t: (b, t, 0))


def _ada_kernel(c_ref, w_ref, b_ref, o_ref):
    s = _silu(c_ref[...]).astype(BF16)
    o_ref[0] = _dot(s, w_ref[0].astype(BF16)) + b_ref[0]


def _ada_modulation(cond, w_ada, b_ada):
    depth, _, n = w_ada.shape
    tn = 1024
    return pl.pallas_call(
        _ada_kernel,
        grid=(depth, n // tn),
        in_specs=[pl.BlockSpec((COND_ROWS, D_MODEL), lambda l, j: (0, 0)),
                  pl.BlockSpec((1, D_MODEL, tn), lambda l, j: (l, 0, j)),
                  pl.BlockSpec((1, 1, tn), lambda l, j: (l, 0, j))],
        out_specs=pl.BlockSpec((1, COND_ROWS, tn), lambda l, j: (l, 0, j)),
        out_shape=jax.ShapeDtypeStruct((depth, COND_ROWS, n), F32),
        compiler_params=_cparams(),
        name="ada_modulation",
    )(cond, w_ada, b_ada.reshape(depth, 1, n))


def _ffn_kernel(xp_ref, xm_ref, xn_ref, mod_ref, gain_ref, wup_ref, cw_ref, cb_ref, wdn_ref, gfin_ref,
                out_ref, hb_ref, z_ref, *, rows, final):
    mod = mod_ref[0]
    shift, scale, gate = _mod_part(mod, 3), _mod_part(mod, 4), _mod_part(mod, 5)
    _fill_halo_block(hb_ref, xp_ref, xm_ref, xn_ref, gain_ref[...], scale, shift, rows)
    for c in range(FFN_DIM // FFN_TILE):
        ca = slice(c * FFN_TILE, (c + 1) * FFN_TILE)
        cg = slice(FFN_DIM + c * FFN_TILE, FFN_DIM + (c + 1) * FFN_TILE)
        a = _dwconv3(_dot(hb_ref[...], wup_ref[:, ca]), cw_ref[:, ca], cb_ref[:, ca], rows)
        g = _dwconv3(_dot(hb_ref[...], wup_ref[:, cg]), cw_ref[:, cg], cb_ref[:, cg], rows)
        z_ref[:, ca] = (_silu(g) * a).astype(BF16)
    y = xm_ref[0] + gate * _dot(z_ref[...], wdn_ref[...])
    if final:
        y = _rms(y, gfin_ref[...])
    out_ref[0] = y


def _ffn_weights(w_up, conv_w, conv_b, w_down):
    return w_up.astype(BF16), conv_w, conv_b.reshape(1, -1), w_down.astype(BF16)


def _conv_ffn(x, mods, cond_row, gain, weights, gain_final, *, final):
    bsz, n, _ = x.shape
    rows = _block_rows(n, FFN_ROWS)
    wup, cw, cb, wdn = weights
    prev, main, nxt = _halo_specs(n, rows)
    return pl.pallas_call(
        functools.partial(_ffn_kernel, rows=rows, final=final),
        grid=(bsz, n // rows),
        in_specs=[prev, main, nxt, _mod_spec(cond_row), _resident((1, D_MODEL)),
                  _resident(wup.shape), _resident(cw.shape), _resident(cb.shape), _resident(wdn.shape),
                  _resident((1, D_MODEL))],
        out_specs=_row_spec(rows, D_MODEL),
        out_shape=jax.ShapeDtypeStruct((bsz, n, D_MODEL), F32),
        scratch_shapes=[pltpu.VMEM((rows + 2 * HALO, D_MODEL), BF16),
                        pltpu.VMEM((rows, FFN_DIM), BF16)],
        compiler_params=_cparams(),
        name="conv_ffn",
    )(x, x, x, mods, gain.reshape(1, D_MODEL), wup, cw, cb, wdn, gain_final.reshape(1, D_MODEL))


def _mlstm_out_kernel(x_ref, hft_ref, hbt_ref, ot_ref, mod_ref, w_ref, out_ref):
    h_sum = hft_ref[0].astype(F32) + hbt_ref[0].astype(F32)
    zt = (h_sum * jax.nn.sigmoid(ot_ref[0].astype(F32))).astype(BF16)
    out_ref[0] = x_ref[0] + _mod_part(mod_ref[0], 2) * _dot_tn(zt, w_ref[...])


def _gla_out_kernel(x_ref, of_ref, ob_ref, r_ref, gn_ref, mod_ref, w_ref, out_ref):
    oo = of_ref[0].astype(F32) + ob_ref[0].astype(F32)
    parts = [_rms(oo[:, h * B_DV:(h + 1) * B_DV], gn_ref[...]) for h in range(B_HEADS)]
    z = (jnp.concatenate(parts, axis=1) * _silu(r_ref[0].astype(F32))).astype(BF16)
    out_ref[0] = x_ref[0] + _mod_part(mod_ref[0], 2) * _dot(z, w_ref[...])


def _na_out_kernel(x_ref, o_ref, mod_ref, w_ref, out_ref):
    out_ref[0] = x_ref[0] + _mod_part(mod_ref[0], 2) * _dot(o_ref[0], w_ref[...])


def _na_out_t_kernel(x_ref, ot_ref, mod_ref, w_ref, out_ref):
    out_ref[0] = x_ref[0] + _mod_part(mod_ref[0], 2) * _dot_tn(ot_ref[0], w_ref[...])


def _mixer_out(kernel_fn, name, x, act_specs, acts, consts, mods, cond_row, w_out):
    bsz, n, _ = x.shape
    rows = _block_rows(n)
    w = w_out.astype(BF16)
    specs = [pl.BlockSpec((1, a.shape[1], rows), lambda b, t: (b, 0, t)) if s == "t"
             else _row_spec(rows, a.shape[-1]) for s, a in zip(act_specs, acts)]
    return pl.pallas_call(
        kernel_fn,
        grid=(bsz, n // rows),
        in_specs=([_row_spec(rows, D_MODEL)] + specs + [_resident(c.shape) for c in consts]
                  + [_mod_spec(cond_row), _resident(w.shape)]),
        out_specs=_row_spec(rows, D_MODEL),
        out_shape=jax.ShapeDtypeStruct((bsz, n, D_MODEL), F32),
        compiler_params=_cparams(),
        name=name,
    )(x, *acts, *consts, mods, w)


def _mlstm_proj_kernel(xp_ref, xm_ref, xn_ref, mod_ref, gain_ref, wqk_ref, cw_ref, cb_ref, wvot_ref, wg_ref,
                       bg_ref, wgt_ref, bgt_ref, q_ref, k_ref, vt_ref, ot_ref, bc_ref, gr_ref, bm_ref, hb_ref,
                       *, rows):
    mod = mod_ref[0]
    shift, scale = _mod_part(mod, 0), _mod_part(mod, 1)
    _fill_halo_block(hb_ref, xp_ref, xm_ref, xn_ref, gain_ref[...], scale, shift, rows)
    hm = hb_ref[HALO:HALO + rows]
    L, H = A_CHUNK, A_HEADS
    gates_c = _dot(hm, wg_ref[...]) + bg_ref[...]
    gates_r = _dot_nt(wgt_ref[...], hm) + bgt_ref[...]
    key = lax.broadcasted_iota(jnp.int32, (L, L), 0)
    qry = lax.broadcasted_iota(jnp.int32, (L, L), 1)
    fwd_mask = jnp.where(qry <= key, 1.0, 0.0).astype(BF16)
    bwd_mask = jnp.where(qry >= key, 1.0, 0.0).astype(BF16)
    fwd_lane = lax.broadcasted_iota(jnp.int32, (L, GATE_PAD), 1) < H
    fwd_row = lax.broadcasted_iota(jnp.int32, (2 * H, L), 0) < H
    lane = lax.broadcasted_iota(jnp.int32, (2 * H, L), 1)
    spans = [slice(c * L, (c + 1) * L) for c in range(rows // L)]
    run_max = []
    for cs in spans:
        logf_c = _log_sigmoid(gates_c[cs, GATE_PAD:])
        g_c = jnp.where(fwd_lane, _cumsum_rows(fwd_mask, logf_c), _cumsum_rows(bwd_mask, logf_c))
        bc_ref[0, cs, :] = gates_c[cs, :GATE_PAD] - g_c
        logf_r = _log_sigmoid(gates_r[2 * H:, cs])
        g_r = jnp.where(fwd_row, _cumsum_cols(logf_r, fwd_mask), _cumsum_cols(logf_r, bwd_mask))
        gr_ref[0, :, cs] = g_r
        run_max.append(gates_r[:2 * H, cs] - g_r)
    dist = 1
    while dist < L:
        for c, rm in enumerate(run_max):
            seen_f = jnp.where(lane >= dist, pltpu.roll(rm, dist, 1), -jnp.inf)
            seen_b = jnp.where(lane < L - dist, pltpu.roll(rm, L - dist, 1), -jnp.inf)
            run_max[c] = jnp.maximum(rm, jnp.where(fwd_row, seen_f, seen_b))
        dist *= 2
    for cs, rm in zip(spans, run_max):
        bm_ref[0, :, cs] = rm

    qk = _silu(_dwconv3(_dot(hb_ref[...], wqk_ref[...]), cw_ref[...], cb_ref[...], rows))
    q_ref[0] = (qk[:, :A_QK] * (A_DK ** -0.5)).astype(BF16)
    k_ref[0] = qk[:, A_QK:].astype(BF16)
    vot = _dot_nt(wvot_ref[...], hm)
    vt_ref[0] = vot[:A_V].astype(BF16)
    ot_ref[0] = vot[A_V:].astype(BF16)


def _mlstm_weights(w_in, b_gate, conv_w, conv_b):
    hh = A_HEADS
    wqk = w_in[:, :2 * A_QK].astype(BF16)
    wvot = w_in[:, 2 * A_QK:2 * A_QK + 2 * A_V].T.astype(BF16)
    wgate = w_in[:, 2 * A_QK + 2 * A_V:]
    pad = GATE_PAD - 2 * hh

    def regroup(t, axis, pad):
        i_f, f_f, i_b, f_b = jnp.split(t, 4, axis=axis)
        widths = [(0, 0)] * (t.ndim - 1) + [(0, pad)]
        return jnp.concatenate([jnp.pad(jnp.concatenate([i_f, i_b], axis), widths),
                                jnp.pad(jnp.concatenate([f_f, f_b], axis), widths)], axis)

    wg = regroup(wgate, 1, pad).astype(BF16)
    bg = regroup(b_gate, 0, pad).reshape(1, 2 * GATE_PAD)
    wgt = regroup(wgate, 1, 0).T.astype(BF16)
    bgt = regroup(b_gate, 0, 0).reshape(4 * hh, 1)
    return wqk, conv_w, conv_b.reshape(1, -1), wvot, wg, bg, wgt, bgt


def _mlstm_proj(x, mods, cond_row, gain, weights):
    bsz, n, _ = x.shape
    rows = _block_rows(n)
    prev, main, nxt = _halo_specs(n, rows)
    t_spec = pl.BlockSpec((1, A_V, rows), lambda b, t: (b, 0, t))
    g_spec = pl.BlockSpec((1, 2 * A_HEADS, rows), lambda b, t: (b, 0, t))
    g_shape = jax.ShapeDtypeStruct((bsz, 2 * A_HEADS, n), F32)
    return pl.pallas_call(
        functools.partial(_mlstm_proj_kernel, rows=rows),
        grid=(bsz, n // rows),
        in_specs=[prev, main, nxt, _mod_spec(cond_row), _resident((1, D_MODEL))]
                 + [_resident(w.shape) for w in weights],
        out_specs=[_row_spec(rows, A_QK), _row_spec(rows, A_QK), t_spec, t_spec,
                   _row_spec(rows, GATE_PAD), g_spec, g_spec],
        out_shape=[jax.ShapeDtypeStruct((bsz, n, A_QK), BF16), jax.ShapeDtypeStruct((bsz, n, A_QK), BF16),
                   jax.ShapeDtypeStruct((bsz, A_V, n), BF16), jax.ShapeDtypeStruct((bsz, A_V, n), BF16),
                   jax.ShapeDtypeStruct((bsz, n, GATE_PAD), F32), g_shape, g_shape],
        scratch_shapes=[pltpu.VMEM((rows + 2 * HALO, D_MODEL), BF16)],
        compiler_params=_cparams(),
        name="mlstm_proj",
    )(x, x, x, mods, gain.reshape(1, D_MODEL), *weights)


def _mlstm_scan_kernel(q_ref, k_ref, vt_ref, bc_ref, gr_ref, bm_ref, s0_ref, m0_ref,
                       ht_ref, s_out_ref, m_out_ref, state_ref, m_ref, *, reverse, cps):
    L, H = A_CHUNK, A_HEADS
    step = pl.program_id(1)

    @pl.when(step == 0)
    def _():
        state_ref[...] = s0_ref[0]
        m_ref[...] = m0_ref[0]

    key = lax.broadcasted_iota(jnp.int32, (L, L), 0)
    qry = lax.broadcasted_iota(jnp.int32, (L, L), 1)
    keep_t = (key >= qry) if reverse else (key <= qry)
    lane0 = H if reverse else 0
    end = 0 if reverse else L - 1

    order = range(cps - 1, -1, -1) if reverse else range(cps)
    span = lambda c: slice(c * L, (c + 1) * L)
    head_sub = lax.broadcasted_iota(jnp.int32, A_MSTATE, 0)
    head_lane = lax.broadcasted_iota(jnp.int32, A_MSTATE, 1)

    m_prev = m_ref[...]
    mx, inter, floor, decay, b_c, w_c = {}, {}, {}, {}, {}, {}
    for c in order:
        cum_r = gr_ref[0, lane0:lane0 + H, span(c)]
        run_max = bm_ref[0, lane0:lane0 + H, span(c)]
        mx[c] = jnp.maximum(run_max, m_prev)
        mx_end = jnp.broadcast_to(mx[c][:, end:end + 1], A_MSTATE)
        g_tot = jnp.broadcast_to(cum_r[:, end:end + 1], A_MSTATE)
        inter[c] = jnp.exp(m_prev - mx[c])
        floor[c] = jnp.exp(-(cum_r + mx[c]))
        decay[c] = jnp.exp(m_prev - mx_end)
        mx_end_lanes = jnp.sum(jnp.where(head_lane == head_sub + lane0, mx_end, 0.0), axis=0, keepdims=True)
        b_c[c] = bc_ref[0, span(c), :]
        w_c[c] = jnp.exp(b_c[c] - mx_end_lanes)
        m_prev = g_tot + mx_end
    m_ref[...] = m_prev

    items = [(c, h) for c in range(cps) for h in range(H)]
    pairs = [(c, p) for c in range(cps) for p in range(H // 2)]
    ones = jnp.ones((A_ONES, L), BF16)
    low = lax.broadcasted_iota(jnp.int32, (L, 2 * A_DK), 1) < A_DK
    low_row = low[0:1]
    k2 = {(c, p): k_ref[0, span(c), 2 * p * A_DK:2 * (p + 1) * A_DK] for c, p in pairs}
    q2 = {(c, p): q_ref[0, span(c), 2 * p * A_DK:2 * (p + 1) * A_DK] for c, p in pairs}
    q_bd = {i: jnp.concatenate([jnp.where(low, q2[i], 0), jnp.where(low, 0, q2[i])], axis=0).astype(BF16)
            for i in pairs}
    vaug_t = {(c, h): jnp.concatenate([vt_ref[0, h * A_DV:(h + 1) * A_DV, span(c)], ones], axis=0)
              for c, h in items}
    s_pair = {i: _dot_nt(k2[i], q_bd[i]) for i in pairs}
    state_add = {}
    for c, p in pairs:
        col = lambda h: w_c[c][:, lane0 + h:lane0 + h + 1]
        kw = (k2[c, p].astype(F32) * jnp.where(low, col(2 * p), col(2 * p + 1))).astype(BF16)
        state_add[c, p] = (_dot(vaug_t[c, 2 * p], jnp.where(low, kw, 0))
                           + _dot(vaug_t[c, 2 * p + 1], jnp.where(low, 0, kw)))
    p_t = {}
    for c, h in items:
        b_col = b_c[c][:, lane0 + h:lane0 + h + 1]
        w_t = jnp.exp(jnp.where(keep_t, b_col - mx[c][h:h + 1, :], -jnp.inf))
        p_t[c, h] = (s_pair[c, h // 2][:, (h % 2) * L:(h % 2 + 1) * L] * w_t).astype(BF16)
    local_t = {i: _dot(vaug_t[i], p_t[i]) for i in items}

    for p in range(H // 2):
        state = state_ref[p]
        for c in order:
            carry = _dot_nt(state.astype(BF16), q_bd[c, p])
            for h in (2 * p, 2 * p + 1):
                tot = local_t[c, h] + inter[c][h:h + 1, :] * carry[:, (h % 2) * L:(h % 2 + 1) * L]
                den = jnp.maximum(jnp.abs(tot[A_DV:A_DV + 1, :]), floor[c][h:h + 1, :])
                ht_ref[0, h * A_DV:(h + 1) * A_DV, span(c)] = (tot[:A_DV] / den).astype(BF16)
            scale = jnp.where(low_row, decay[c][2 * p:2 * p + 1, :], decay[c][2 * p + 1:2 * p + 2, :])
            state = scale * state + state_add[c, p]
        state_ref[p] = state

    @pl.when(step == pl.num_programs(1) - 1)
    def _():
        s_out_ref[0] = state_ref[...]
        m_out_ref[0] = m_ref[...]


def _chunk_order(n_chunks, reverse):
    return (lambda c: n_chunks - 1 - c) if reverse else (lambda c: c)


def _mlstm_scan(q, k, vt, b_c, g_r, bmax_r, state0, m0, *, reverse):
    bsz, n, _ = q.shape
    cps = min(A_CHUNKS_PER_STEP, n // A_CHUNK)
    L = A_CHUNK * cps
    cidx = _chunk_order(n // L, reverse)
    tok = lambda w: pl.BlockSpec((1, L, w), lambda b, c: (b, cidx(c), 0))
    chan = lambda r: pl.BlockSpec((1, r, L), lambda b, c: (b, 0, cidx(c)))
    s_spec = pl.BlockSpec((1,) + A_STATE, lambda b, c: (b, 0, 0, 0))
    m_spec = pl.BlockSpec((1,) + A_MSTATE, lambda b, c: (b, 0, 0))
    return pl.pallas_call(
        functools.partial(_mlstm_scan_kernel, reverse=reverse, cps=cps),
        grid=(bsz, n // L),
        in_specs=[tok(A_QK), tok(A_QK), chan(A_V), tok(GATE_PAD),
                  chan(2 * A_HEADS), chan(2 * A_HEADS), s_spec, m_spec],
        out_specs=[chan(A_V), s_spec, m_spec],
        out_shape=[jax.ShapeDtypeStruct((bsz, A_V, n), BF16),
                   jax.ShapeDtypeStruct((bsz,) + A_STATE, F32),
                   jax.ShapeDtypeStruct((bsz,) + A_MSTATE, F32)],
        scratch_shapes=[pltpu.VMEM(A_STATE, F32), pltpu.VMEM(A_MSTATE, F32)],
        compiler_params=_cparams(),
        name="mlstm_scan_bwd" if reverse else "mlstm_scan_fwd",
    )(q, k, vt, b_c, g_r, bmax_r, state0, m0)


def _mlstm_mixer(x_ctx, x_lat, mods, ctx_row, gain, w_in, b_gate, conv_w, conv_b):
    weights = _mlstm_weights(w_in, b_gate, conv_w, conv_b)
    bsz = x_lat.shape[0]
    streams = []
    for x, row in ((x_ctx, ctx_row), (x_lat, None)):
        q, k, vt, ot, b_c, g_r, bmax_r = _mlstm_proj(x, mods, row, gain, weights)
        streams.append(((q, k, vt, b_c, g_r, bmax_r), ot))
    outs = []
    for reverse in (False, True):
        state = jnp.zeros((bsz,) + A_STATE, F32)
        m = jnp.zeros((bsz,) + A_MSTATE, F32)
        res = []
        for args, _ in streams:
            ht, state, m = _mlstm_scan(*args, state, m, reverse=reverse)
            res.append(ht)
        outs.append(res)
    (hcf, hlf), (hcb, hlb) = outs
    return (hcf, hcb, streams[0][1]), (hlf, hlb, streams[1][1])


def _gla_proj_kernel(x_ref, mod_ref, gain_ref, wqkv_ref, wr_ref, wa_ref, wa2_ref, ba_ref,
                     q_ref, k_ref, v_ref, r_ref, gf_ref, gb_ref, *, rows):
    mod = mod_ref[0]
    h = _norm_mod(x_ref[0], gain_ref[...], _mod_part(mod, 1), _mod_part(mod, 0)).astype(BF16)
    L = B_CHUNK
    a = _dot(h, wa_ref[...]).astype(BF16)
    row = lax.broadcasted_iota(jnp.int32, (L, L), 0)
    col = lax.broadcasted_iota(jnp.int32, (L, L), 1)
    for d, out_ref in enumerate((gf_ref, gb_ref)):
        mask = jnp.where((col >= row) if d else (col <= row), 1.0, 0.0).astype(BF16)
        g = _log_sigmoid(_dot(a, wa2_ref[d]) + ba_ref[d]) / B_TAU
        for c in range(rows // L):
            out_ref[0, c * L:(c + 1) * L, :] = _cumsum_rows(mask, g[c * L:(c + 1) * L])
    qkv = _dot(h, wqkv_ref[...])
    q_ref[0] = qkv[:, :B_QK].astype(BF16)
    k_ref[0] = qkv[:, B_QK:2 * B_QK].astype(BF16)
    v_ref[0] = qkv[:, 2 * B_QK:].astype(BF16)
    r_ref[0] = _dot(h, wr_ref[...]).astype(BF16)


def _gla_proj(x, mods, cond_row, gain, weights):
    bsz, n, _ = x.shape
    rows = _block_rows(n)
    return pl.pallas_call(
        functools.partial(_gla_proj_kernel, rows=rows),
        grid=(bsz, n // rows),
        in_specs=[_row_spec(rows, D_MODEL), _mod_spec(cond_row), _resident((1, D_MODEL))]
                 + [_resident(w.shape) for w in weights],
        out_specs=[_row_spec(rows, B_QK), _row_spec(rows, B_QK), _row_spec(rows, B_V), _row_spec(rows, B_V),
                   _row_spec(rows, B_QK), _row_spec(rows, B_QK)],
        out_shape=[jax.ShapeDtypeStruct((bsz, n, B_QK), BF16), jax.ShapeDtypeStruct((bsz, n, B_QK), BF16),
                   jax.ShapeDtypeStruct((bsz, n, B_V), BF16), jax.ShapeDtypeStruct((bsz, n, B_V), BF16),
                   jax.ShapeDtypeStruct((bsz, n, B_QK), F32), jax.ShapeDtypeStruct((bsz, n, B_QK), F32)],
        compiler_params=_cparams(),
        name="gla_proj",
    )(x, mods, gain.reshape(1, D_MODEL), *weights)


def _gla_scan_kernel(q_ref, k_ref, v_ref, g_ref, s0_ref, o_ref, s_out_ref, state_ref, *, reverse, cps):
    L, sub = B_CHUNK, B_SUB
    nsub = L // sub
    step = pl.program_id(1)

    @pl.when(step == 0)
    def _():
        state_ref[...] = s0_ref[0]

    pair = lax.broadcasted_iota(jnp.int32, (L * sub, 128), 0)
    lane = lax.broadcasted_iota(jnp.int32, (L * sub, 128), 1)
    pair_q, pair_s = pair // sub, pair % sub
    place = lane == (pair_q // sub) * sub + pair_s
    sel = (lax.broadcasted_iota(jnp.int32, (L, L * sub), 1) // sub
           == lax.broadcasted_iota(jnp.int32, (L, L * sub), 0))
    sel_bf16 = jnp.where(sel, 1.0, 0.0).astype(BF16)
    ones_red = jnp.ones((B_DK, 128), BF16)
    srow = lax.broadcasted_iota(jnp.int32, (sub, B_DK), 0)

    items = [(c, h) for c in range(cps) for h in range(B_HEADS)]
    rows = lambda c: slice(c * L, (c + 1) * L)
    Gs = {(c, h): g_ref[0, rows(c), h * B_DK:(h + 1) * B_DK] for c, h in items}
    qs = {(c, h): q_ref[0, rows(c), h * B_DK:(h + 1) * B_DK].astype(F32) * (B_DK ** -0.5) for c, h in items}
    ks = {(c, h): k_ref[0, rows(c), h * B_DK:(h + 1) * B_DK].astype(F32) for c, h in items}
    vs = {(c, h): v_ref[0, rows(c), h * B_DV:(h + 1) * B_DV] for c, h in items}
    g_tot = {i: (G[0:1] if reverse else G[L - 1:L]) for i, G in Gs.items()}
    q_carry = {i: (qs[i] * jnp.exp(Gs[i])).astype(BF16) for i in items}
    state_add = {i: _dot_tn(vs[i], (ks[i] * jnp.exp(g_tot[i] - Gs[i])).astype(BF16)) for i in items}

    def cross_scores(G, qh, kh):
        def edge(b):
            r = b * sub if reverse else b * sub + sub - 1
            return G[r:r + 1]
        edges = jnp.concatenate([jnp.broadcast_to(edge(b), (sub, B_DK)) for b in range(nsub)], axis=0)
        k_edge = kh * jnp.exp(edges - G)
        q_cols, k_cols = [], []
        for b in range(nsub):
            lo, hi = b * sub, (b + 1) * sub
            before, after = jnp.zeros((lo, B_DK), F32), jnp.zeros((L - hi, B_DK), F32)
            if reverse:
                seen = [qh[:lo] * jnp.exp(G[:lo] - edge(b)), jnp.zeros((L - lo, B_DK), F32)]
            else:
                seen = [jnp.zeros((hi, B_DK), F32), qh[hi:] * jnp.exp(G[hi:] - edge(b))]
            q_cols.append(jnp.concatenate([s for s in seen if s.shape[0]], axis=0).astype(BF16))
            k_parts = [before, k_edge[lo:hi], after]
            k_cols.append(jnp.concatenate([s for s in k_parts if s.shape[0]], axis=0).astype(BF16))
        return _dot_nt(jnp.concatenate(q_cols, axis=1), jnp.concatenate(k_cols, axis=1))

    att_cross = {i: cross_scores(Gs[i], qs[i], ks[i]) for i in items}

    def pair_sums(G, qh, kh):
        pieces = []
        for b in range(nsub):
            Gb = G[b * sub:(b + 1) * sub]
            kb = kh[b * sub:(b + 1) * sub]
            for j in range(sub):
                r = b * sub + j
                vis = (srow >= j) if reverse else (srow <= j)
                pieces.append(qh[r:r + 1] * kb * jnp.exp(jnp.where(vis, G[r:r + 1] - Gb, -jnp.inf)))
        pair_terms = jnp.concatenate(pieces, axis=0).astype(BF16)
        return _dot(pair_terms, ones_red)

    sums = {i: pair_sums(Gs[i], qs[i], ks[i]) for i in items}
    att_local = {i: _dot(sel_bf16, jnp.where(place, sums[i], 0.0).astype(BF16)) for i in items}
    o_local = {i: _dot((att_cross[i] + att_local[i][:, :L]).astype(BF16), vs[i]) for i in items}

    order = range(cps - 1, -1, -1) if reverse else range(cps)
    for h in range(B_HEADS):
        state = state_ref[h]
        for c in order:
            o_ref[0, rows(c), h * B_DV:(h + 1) * B_DV] = (
                o_local[c, h] + _dot_nt(q_carry[c, h], state.astype(BF16))).astype(BF16)
            state = state * jnp.exp(g_tot[c, h]) + state_add[c, h]
        state_ref[h] = state

    @pl.when(step == pl.num_programs(1) - 1)
    def _():
        s_out_ref[0] = state_ref[...]


def _gla_scan(q, k, v, g, state0, *, reverse):
    bsz, n, _ = q.shape
    L = B_CHUNK * B_CHUNKS_PER_STEP
    cidx = _chunk_order(n // L, reverse)
    sshape = (B_HEADS, B_DV, B_DK)
    return pl.pallas_call(
        functools.partial(_gla_scan_kernel, reverse=reverse, cps=B_CHUNKS_PER_STEP),
        grid=(bsz, n // L),
        in_specs=[pl.BlockSpec((1, L, B_QK), lambda b, c: (b, cidx(c), 0)),
                  pl.BlockSpec((1, L, B_QK), lambda b, c: (b, cidx(c), 0)),
                  pl.BlockSpec((1, L, B_V), lambda b, c: (b, cidx(c), 0)),
                  pl.BlockSpec((1, L, B_QK), lambda b, c: (b, cidx(c), 0)),
                  pl.BlockSpec((1,) + sshape, lambda b, c: (b, 0, 0, 0))],
        out_specs=[pl.BlockSpec((1, L, B_V), lambda b, c: (b, cidx(c), 0)),
                   pl.BlockSpec((1,) + sshape, lambda b, c: (b, 0, 0, 0))],
        out_shape=[jax.ShapeDtypeStruct((bsz, n, B_V), BF16), jax.ShapeDtypeStruct((bsz,) + sshape, F32)],
        scratch_shapes=[pltpu.VMEM(sshape, F32)],
        compiler_params=_cparams(),
        name="gla_scan_bwd" if reverse else "gla_scan_fwd",
    )(q, k, v, g, state0)


def _gla_mixer(x_ctx, x_lat, mods, ctx_row, gain, w_in, w_a2, b_a):
    wqkv = w_in[:, :2 * B_QK + B_V].astype(BF16)
    wr = w_in[:, 2 * B_QK + B_V:2 * B_QK + 2 * B_V].astype(BF16)
    wa = jnp.pad(w_in[:, 2 * B_QK + 2 * B_V:], ((0, 0), (0, GATE_PAD - 2 * B_RANK))).astype(BF16)
    wa2 = jnp.stack([jnp.zeros((GATE_PAD, B_QK), F32).at[d * B_RANK:(d + 1) * B_RANK].set(w_a2[d])
                     for d in range(2)]).astype(BF16)
    ba = b_a.reshape(2, 1, B_QK)
    bsz = x_lat.shape[0]
    pc = _gla_proj(x_ctx, mods, ctx_row, gain, (wqkv, wr, wa, wa2, ba))
    plat = _gla_proj(x_lat, mods, None, gain, (wqkv, wr, wa, wa2, ba))
    outs = []
    for d, reverse in enumerate((False, True)):
        state = jnp.zeros((bsz, B_HEADS, B_DV, B_DK), F32)
        res = []
        for p in (pc, plat):
            o, state = _gla_scan(p[0], p[1], p[2], p[4 + d], state, reverse=reverse)
            res.append(o)
        outs.append(res)
    (ocf, olf), (ocb, olb) = outs
    return (ocf, ocb, pc[3]), (olf, olb, plat[3])


def _na_proj_kernel_ctx(x_ref, mod_ref, gain_ref, w_ref, wvt_ref, q_ref, k_ref, v_ref, vt_ref):
    mod = mod_ref[0]
    h = _norm_mod(x_ref[0], gain_ref[...], _mod_part(mod, 1), _mod_part(mod, 0)).astype(BF16)
    qkv = _dot(h, w_ref[...])
    q_ref[0] = (qkv[:, :D_MODEL] * (C_DH ** -0.5)).astype(BF16)
    k_ref[0] = qkv[:, D_MODEL:2 * D_MODEL].astype(BF16)
    v_ref[0] = qkv[:, 2 * D_MODEL:].astype(BF16)
    vt_ref[0] = _dot_nt(wvt_ref[...], h).astype(BF16)


def _na_proj_kernel_lat(x_ref, mod_ref, gain_ref, w_ref, wvt_ref, q_ref, k_ref, vt_ref):
    mod = mod_ref[0]
    h = _norm_mod(x_ref[0], gain_ref[...], _mod_part(mod, 1), _mod_part(mod, 0)).astype(BF16)
    qk = _dot(h, w_ref[...])
    q_ref[0] = (qk[:, :D_MODEL] * (C_DH ** -0.5)).astype(BF16)
    k_ref[0] = qk[:, D_MODEL:].astype(BF16)
    vt_ref[0] = _dot_nt(wvt_ref[...], h).astype(BF16)


def _na_proj(x, mods, cond_row, gain, w_in, *, with_v):
    bsz, n, _ = x.shape
    rows = _block_rows(n)
    wvt = w_in[:, 2 * D_MODEL:].T.astype(BF16)
    w = (w_in if with_v else w_in[:, :2 * D_MODEL]).astype(BF16)
    tok = jax.ShapeDtypeStruct((bsz, n, D_MODEL), BF16)
    vt_shape = jax.ShapeDtypeStruct((bsz, D_MODEL, n), BF16)
    vt_spec = pl.BlockSpec((1, D_MODEL, rows), lambda b, t: (b, 0, t))
    rs = _row_spec(rows, D_MODEL)
    return pl.pallas_call(
        _na_proj_kernel_ctx if with_v else _na_proj_kernel_lat,
        grid=(bsz, n // rows),
        in_specs=[rs, _mod_spec(cond_row), _resident((1, D_MODEL)), _resident(w.shape), _resident(wvt.shape)],
        out_specs=[rs, rs, rs, vt_spec] if with_v else [rs, rs, vt_spec],
        out_shape=[tok, tok, tok, vt_shape] if with_v else [tok, tok, vt_shape],
        compiler_params=_cparams(),
        name="na_proj_ctx" if with_v else "na_proj_lat",
    )(x, mods, gain.reshape(1, D_MODEL), w, wvt)


def _na_bias_table(rpb):
    colq = np.arange(GRID_W)[None, :]
    colk = np.arange(GRID_W)[:, None]
    start = np.clip(colq - C_WIN_COLS // 2, 0, GRID_W - C_WIN_COLS)
    valid = (colk >= start) & (colk < start + C_WIN_COLS)
    w2 = 2 * GRID_W
    lead = GRID_W - C_WIN_COLS
    padded = jnp.pad(rpb, ((0, 0), (0, 0), (lead, w2 - lead - rpb.shape[-1])))
    skew = jnp.tile(padded, (1, 1, GRID_W))[..., :GRID_W * (w2 - 1)]
    skew = skew.reshape(rpb.shape[0], rpb.shape[1], GRID_W, w2 - 1)[..., GRID_W - 1:]
    bias = jnp.where(valid[None, None], jnp.swapaxes(skew, 2, 3), -jnp.inf)
    bias = bias.transpose(1, 2, 0, 3).reshape(2 * C_WIN_ROWS - 1, GRID_W, C_HEADS * GRID_W)
    return jnp.concatenate([jnp.full((1,) + bias.shape[1:], -jnp.inf, F32), bias], axis=0)


def _na_lat_kernel(q_ref, k_ref, kc_ref, vt_ref, vtc_ref, tab_ref, ot_ref, *, rows):
    r0 = 2 * pl.program_id(1)
    gw = C_GROUP * C_DH
    nk = C_KROWS * GRID_W
    first = lambda r: jnp.clip(r - C_WIN_ROWS // 2, 0, rows - C_WIN_ROWS)
    rs_load = jnp.minimum(first(r0) - first(r0) % 2, rows - C_KROWS)
    start = pl.multiple_of(rs_load * GRID_W, 2 * GRID_W)
    entries = []
    for i in range(C_KROWS):
        kr = rs_load + i
        per_q = []
        for r in (r0, r0 + 1):
            ok = jnp.logical_and(kr >= first(r), kr < first(r) + C_WIN_ROWS)
            per_q.append(jnp.where(ok, kr - r + C_WIN_ROWS, 0))
        entries.append(per_q)

    blk_r = lax.broadcasted_iota(jnp.int32, (gw, gw), 0) // C_DH
    blk_c = lax.broadcasted_iota(jnp.int32, (gw, gw), 1) // C_DH
    same_head = blk_r == blk_c

    def block_diag(qg):
        return jnp.where(same_head, jnp.concatenate([qg] * C_GROUP, axis=0), 0).astype(BF16)

    groups = range(C_HEADS // C_GROUP)
    gsl = [slice(g * gw, (g + 1) * gw) for g in groups]
    q_bd = [jnp.concatenate([block_diag(q_ref[0, :GRID_W, gs]), block_diag(q_ref[0, GRID_W:, gs])], axis=0)
            for gs in gsl]
    s_loc = [_dot_nt(k_ref[0, pl.ds(start, nk), gs], qb) for gs, qb in zip(gsl, q_bd)]
    s_ctx = [_dot_nt(kc_ref[0, :, gs], qb) for gs, qb in zip(gsl, q_bd)]
    p_loc, p_ctx, denom = [], [], []
    for g in groups:
        bias = jnp.concatenate(
            [jnp.concatenate([tab_ref[e, :, gsl[g]] for e in per_q], axis=1) for per_q in entries], axis=0)
        sl = s_loc[g] + bias
        m = jnp.maximum(jnp.max(sl, axis=0, keepdims=True), jnp.max(s_ctx[g], axis=0, keepdims=True))
        pl_, pc_ = jnp.exp(sl - m), jnp.exp(s_ctx[g] - m)
        denom.append(jnp.sum(pl_, axis=0, keepdims=True) + jnp.sum(pc_, axis=0, keepdims=True))
        p_loc.append(pl_.astype(BF16))
        p_ctx.append(pc_.astype(BF16))
    for g in groups:
        ot = (_dot(vt_ref[0, gsl[g], pl.ds(start, nk)], p_loc[g])
              + _dot(vtc_ref[0, gsl[g], :], p_ctx[g])) / denom[g]
        for qr in range(2):
            for h in range(C_GROUP):
                ch = slice(g * gw + h * C_DH, g * gw + (h + 1) * C_DH)
                ot_ref[0, ch, qr * GRID_W:(qr + 1) * GRID_W] = (
                    ot[h * C_DH:(h + 1) * C_DH, qr * gw + h * C_DH:qr * gw + (h + 1) * C_DH].astype(BF16))


def _na_lat_attention(q, k, kc, vt, vtc, tab):
    bsz, n, _ = q.shape
    rows = n // GRID_W
    n_ctx = kc.shape[1]
    assert rows >= C_KROWS and rows % 2 == 0
    return pl.pallas_call(
        functools.partial(_na_lat_kernel, rows=rows),
        grid=(bsz, rows // 2),
        in_specs=[pl.BlockSpec((1, 2 * GRID_W, D_MODEL), lambda b, r: (b, r, 0)),
                  pl.BlockSpec((1, n, D_MODEL), lambda b, r: (b, 0, 0)),
                  pl.BlockSpec((1, n_ctx, D_MODEL), lambda b, r: (b, 0, 0)),
                  pl.BlockSpec((1, D_MODEL, n), lambda b, r: (b, 0, 0)),
                  pl.BlockSpec((1, D_MODEL, n_ctx), lambda b, r: (b, 0, 0)),
                  _resident(tab.shape)],
        out_specs=pl.BlockSpec((1, D_MODEL, 2 * GRID_W), lambda b, r: (b, 0, r)),
        out_shape=jax.ShapeDtypeStruct((bsz, D_MODEL, n), BF16),
        compiler_params=_cparams(),
        name="na_lat_attention",
    )(q, k, kc, vt, vtc, tab)


def _na_ctx_kernel(q_ref, k_ref, v_ref, o_ref):
    for h in range(C_HEADS):
        hs = slice(h * C_DH, (h + 1) * C_DH)
        s = _dot_nt(q_ref[0, :, hs], k_ref[0, :, hs])
        p = jnp.exp(s - jnp.max(s, axis=-1, keepdims=True))
        o = _dot(p.astype(BF16), v_ref[0, :, hs]) / jnp.sum(p, axis=-1, keepdims=True)
        o_ref[0, :, hs] = o.astype(BF16)


def _na_ctx_attention(q, k, v):
    bsz, n, _ = q.shape
    spec = pl.BlockSpec((1, n, D_MODEL), lambda b: (b, 0, 0))
    return pl.pallas_call(
        _na_ctx_kernel,
        grid=(bsz,),
        in_specs=[spec, spec, spec],
        out_specs=spec,
        out_shape=jax.ShapeDtypeStruct((bsz, n, D_MODEL), BF16),
        compiler_params=_cparams(1),
        name="na_ctx_attention",
    )(q, k, v)


def kernel(x, c, ctx, c_ctx, w_ada, b_ada, norm_mix, norm_ffn, w_up, ffn_conv_w, ffn_conv_b, w_down,
           a_w_in, a_b_gate, a_conv_w, a_conv_b, a_w_out,
           b_w_in, b_w_a2, b_b_a, b_norm, b_w_out,
           c_w_in, c_rpb, c_w_out, norm_final):
    bsz = x.shape[0]
    depth = w_ada.shape[0]
    ctx_row = bsz
    assert bsz < COND_ROWS

    cond = jnp.concatenate([c, c_ctx[None], jnp.zeros((COND_ROWS - bsz - 1, D_MODEL), F32)], axis=0)
    mod_all = _ada_modulation(cond, w_ada, b_ada)
    x_ctx, x_lat = ctx, x

    for i in range(depth):
        last = i == depth - 1
        mods = mod_all[i].reshape(COND_ROWS, 1, 6 * D_MODEL)
        kind, j = i % 3, i // 3
        if kind == 0:
            acts_c, acts_l = _mlstm_mixer(x_ctx, x_lat, mods, ctx_row, norm_mix[i], a_w_in[j], a_b_gate[j],
                                          a_conv_w[j], a_conv_b[j])
            out = functools.partial(_mixer_out, _mlstm_out_kernel, "mlstm_out", consts=[], mods=mods,
                                    w_out=a_w_out[j], act_specs="ttt")
        elif kind == 1:
            acts_c, acts_l = _gla_mixer(x_ctx, x_lat, mods, ctx_row, norm_mix[i], b_w_in[j], b_w_a2[j], b_b_a[j])
            out = functools.partial(_mixer_out, _gla_out_kernel, "gla_out", consts=[b_norm[j].reshape(1, B_DV)],
                                    mods=mods, w_out=b_w_out[j], act_specs="rrr")
        else:
            qc, kc, vc, vtc = _na_proj(x_ctx, mods, ctx_row, norm_mix[i], c_w_in[j], with_v=True)
            ql, kl, vtl = _na_proj(x_lat, mods, None, norm_mix[i], c_w_in[j], with_v=False)
            ot = _na_lat_attention(ql, kl, kc, vtl, vtc, _na_bias_table(c_rpb[j]))
            mid_lat = _mixer_out(_na_out_t_kernel, "na_out_lat", x_lat, "t", [ot], [], mods, None, c_w_out[j])
            if not last:
                oc = _na_ctx_attention(qc, kc, vc)
                mid_ctx = _mixer_out(_na_out_kernel, "na_out_ctx", x_ctx, "r", [oc], [], mods, ctx_row, c_w_out[j])
        if kind != 2:
            mid_lat = out(x=x_lat, acts=list(acts_l), cond_row=None)
            if not last:
                mid_ctx = out(x=x_ctx, acts=list(acts_c), cond_row=ctx_row)
        ffn_w = _ffn_weights(w_up[i], ffn_conv_w[i], ffn_conv_b[i], w_down[i])
        x_lat = _conv_ffn(mid_lat, mods, None, norm_ffn[i], ffn_w, norm_final, final=last)
        if not last:
            x_ctx = _conv_ffn(mid_ctx, mods, ctx_row, norm_ffn[i], ffn_w, norm_final, final=False)
    return x_lat
```

```python
import functools

import jax
import jax.numpy as jnp
import numpy as np
from jax import lax
from jax.experimental import pallas as pl
from jax.experimental.pallas import tpu as pltpu

F32 = jnp.float32
BF16 = jnp.bfloat16

D_MODEL = 1024
EPS = 1e-6
GRID_W = 64

HALO = 16
VMEM_LIMIT = 56 * 1024 * 1024
COND_ROWS = 16

A_HEADS, A_DK, A_DV, A_CHUNK = 8, 64, 128, 128
A_QK, A_V = A_HEADS * A_DK, A_HEADS * A_DV
A_ONES = 16
A_CHUNKS_PER_STEP = 4
A_STATE = (A_HEADS // 2, A_DV + A_ONES, 2 * A_DK)
A_MSTATE = (A_HEADS, 128)
B_HEADS, B_DK, B_DV, B_CHUNK = 4, 128, 256, 64
B_QK, B_V, B_RANK, B_TAU = B_HEADS * B_DK, B_HEADS * B_DV, 16, 16.0
B_SUB = 8
B_CHUNKS_PER_STEP = 4
B_MILD_DECAY = 60.0
C_HEADS, C_DH, C_WIN_ROWS, C_WIN_COLS = 16, 64, 8, 16
C_GROUP = 4
C_KROWS = C_WIN_ROWS + 2
FFN_DIM = 2816
FFN_TILE = 256
FFN_ROWS = 1024
GATE_PAD = 128

NT_DIMS = (((1,), (1,)), ((), ()))
TN_DIMS = (((0,), (0,)), ((), ()))


def _cparams(n_axes=2):
    return pltpu.CompilerParams(dimension_semantics=("arbitrary",) * n_axes, vmem_limit_bytes=VMEM_LIMIT)


def _resident(shape):
    nd = len(shape)
    return pl.BlockSpec(shape, lambda *_: (0,) * nd, pipeline_mode=pl.Buffered(1))


def _block_rows(n_tokens, largest=512):
    return max(r for r in (256, 512, 1024) if r <= largest and (n_tokens % r == 0 or r == 256))


def _dot(a, b):
    return jnp.dot(a, b, preferred_element_type=F32)


def _dot_nt(a, b):
    return lax.dot_general(a, b, NT_DIMS, preferred_element_type=F32)


def _dot_tn(a, b):
    return lax.dot_general(a, b, TN_DIMS, preferred_element_type=F32)


def _split3(x):
    x1 = x.astype(BF16)
    r1 = x - x1.astype(F32)
    x2 = r1.astype(BF16)
    x3 = (r1 - x2.astype(F32)).astype(BF16)
    return x1, x2, x3


def _cumsum_rows(mask_bf16, x):
    x1, x2, x3 = _split3(x)
    return _dot(mask_bf16, x1) + _dot(mask_bf16, x2) + _dot(mask_bf16, x3)


def _cumsum_cols(x, mask_bf16):
    x1, x2, x3 = _split3(x)
    return _dot_nt(x1, mask_bf16) + _dot_nt(x2, mask_bf16) + _dot_nt(x3, mask_bf16)


def _log_sigmoid(z):
    return jnp.minimum(z, 0.0) - jnp.log1p(jnp.exp(-jnp.abs(z)))


def _silu(z):
    return z * jax.nn.sigmoid(z)


def _rms(x, gain):
    return x * lax.rsqrt(jnp.mean(x * x, axis=-1, keepdims=True) + EPS) * gain


def _norm_mod(x, gain, scale, shift):
    return _rms(x, gain) * (1.0 + scale) + shift


def _mod_part(mod, i):
    return mod[:, i * D_MODEL:(i + 1) * D_MODEL]


def _dwconv3(u, w, b, rows):
    n = rows + 2 * HALO
    prev = pltpu.roll(u, 1, 0)[HALO:HALO + rows]
    nxt = pltpu.roll(u, n - 1, 0)[HALO:HALO + rows]
    return prev * w[0:1] + u[HALO:HALO + rows] * w[1:2] + nxt * w[2:3] + b


def _fill_halo_block(hb_ref, xp_ref, xm_ref, xn_ref, gain, scale, shift, rows):
    t = pl.program_id(1)
    has_prev = (t != 0).astype(F32)
    has_next = (t != pl.num_programs(1) - 1).astype(F32)
    hb_ref[0:HALO] = (_norm_mod(xp_ref[0], gain, scale, shift) * has_prev).astype(BF16)
    hb_ref[HALO:HALO + rows] = _norm_mod(xm_ref[0], gain, scale, shift).astype(BF16)
    hb_ref[HALO + rows:] = (_norm_mod(xn_ref[0], gain, scale, shift) * has_next).astype(BF16)


def _halo_specs(n_tokens, rows):
    per = rows // HALO
    last = n_tokens // HALO - 1
    prev = pl.BlockSpec((1, HALO, D_MODEL), lambda b, t: (b, jnp.maximum(t * per - 1, 0), 0))
    main = pl.BlockSpec((1, rows, D_MODEL), lambda b, t: (b, t, 0))
    nxt = pl.BlockSpec((1, HALO, D_MODEL), lambda b, t: (b, jnp.minimum((t + 1) * per, last), 0))
    return prev, main, nxt


def _mod_spec(cond_row):
    if cond_row is None:
        return pl.BlockSpec((1, 1, 6 * D_MODEL), lambda b, t: (b, 0, 0))
    return pl.BlockSpec((1, 1, 6 * D_MODEL), lambda b, t: (cond_row, 0, 0))


def _row_spec(rows, width):
    return pl.BlockSpec((1, rows, width), lambda b, t: (b, t, 0))


def _ada_kernel(c_ref, w_ref, b_ref, o_ref):
    s = _silu(c_ref[...]).astype(BF16)
    o_ref[0] = _dot(s, w_ref[0].astype(BF16)) + b_ref[0]


def _ada_modulation(cond, w_ada, b_ada):
    depth, _, n = w_ada.shape
    tn = 1024
    return pl.pallas_call(
        _ada_kernel,
        grid=(depth, n // tn),
        in_specs=[pl.BlockSpec((COND_ROWS, D_MODEL), lambda l, j: (0, 0)),
                  pl.BlockSpec((1, D_MODEL, tn), lambda l, j: (l, 0, j)),
                  pl.BlockSpec((1, 1, tn), lambda l, j: (l, 0, j))],
        out_specs=pl.BlockSpec((1, COND_ROWS, tn), lambda l, j: (l, 0, j)),
        out_shape=jax.ShapeDtypeStruct((depth, COND_ROWS, n), F32),
        compiler_params=_cparams(),
        name="ada_modulation",
    )(cond, w_ada, b_ada.reshape(depth, 1, n))


def _ffn_kernel(xp_ref, xm_ref, xn_ref, mod_ref, gain_ref, wup_ref, cw_ref, cb_ref, wdn_ref, gfin_ref,
                out_ref, hb_ref, z_ref, *, rows, final):
    mod = mod_ref[0]
    shift, scale, gate = _mod_part(mod, 3), _mod_part(mod, 4), _mod_part(mod, 5)
    _fill_halo_block(hb_ref, xp_ref, xm_ref, xn_ref, gain_ref[...], scale, shift, rows)
    for c in range(FFN_DIM // FFN_TILE):
        ca = slice(c * FFN_TILE, (c + 1) * FFN_TILE)
        cg = slice(FFN_DIM + c * FFN_TILE, FFN_DIM + (c + 1) * FFN_TILE)
        a = _dwconv3(_dot(hb_ref[...], wup_ref[:, ca]), cw_ref[:, ca], cb_ref[:, ca], rows)
        g = _dwconv3(_dot(hb_ref[...], wup_ref[:, cg]), cw_ref[:, cg], cb_ref[:, cg], rows)
        z_ref[:, ca] = (_silu(g) * a).astype(BF16)
    y = xm_ref[0] + gate * _dot(z_ref[...], wdn_ref[...])
    if final:
        y = _rms(y, gfin_ref[...])
    out_ref[0] = y


def _ffn_weights(w_up, conv_w, conv_b, w_down):
    return w_up.astype(BF16), conv_w, conv_b.reshape(1, -1), w_down.astype(BF16)


def _conv_ffn(x, mods, cond_row, gain, weights, gain_final, *, final):
    bsz, n, _ = x.shape
    rows = _block_rows(n, FFN_ROWS)
    wup, cw, cb, wdn = weights
    prev, main, nxt = _halo_specs(n, rows)
    return pl.pallas_call(
        functools.partial(_ffn_kernel, rows=rows, final=final),
        grid=(bsz, n // rows),
        in_specs=[prev, main, nxt, _mod_spec(cond_row), _resident((1, D_MODEL)),
                  _resident(wup.shape), _resident(cw.shape), _resident(cb.shape), _resident(wdn.shape),
                  _resident((1, D_MODEL))],
        out_specs=_row_spec(rows, D_MODEL),
        out_shape=jax.ShapeDtypeStruct((bsz, n, D_MODEL), F32),
        scratch_shapes=[pltpu.VMEM((rows + 2 * HALO, D_MODEL), BF16),
                        pltpu.VMEM((rows, FFN_DIM), BF16)],
        compiler_params=_cparams(),
        name="conv_ffn",
    )(x, x, x, mods, gain.reshape(1, D_MODEL), wup, cw, cb, wdn, gain_final.reshape(1, D_MODEL))


def _mlstm_out_kernel(x_ref, hft_ref, hbt_ref, ot_ref, mod_ref, w_ref, out_ref):
    h_sum = hft_ref[0].astype(F32) + hbt_ref[0].astype(F32)
    zt = (h_sum * jax.nn.sigmoid(ot_ref[0].astype(F32))).astype(BF16)
    out_ref[0] = x_ref[0] + _mod_part(mod_ref[0], 2) * _dot_tn(zt, w_ref[...])


def _gla_out_kernel(x_ref, of_ref, ob_ref, r_ref, gn_ref, mod_ref, w_ref, out_ref):
    oo = of_ref[0].astype(F32) + ob_ref[0].astype(F32)
    parts = [_rms(oo[:, h * B_DV:(h + 1) * B_DV], gn_ref[...]) for h in range(B_HEADS)]
    z = (jnp.concatenate(parts, axis=1) * _silu(r_ref[0].astype(F32))).astype(BF16)
    out_ref[0] = x_ref[0] + _mod_part(mod_ref[0], 2) * _dot(z, w_ref[...])


def _na_out_kernel(x_ref, o_ref, mod_ref, w_ref, out_ref):
    out_ref[0] = x_ref[0] + _mod_part(mod_ref[0], 2) * _dot(o_ref[0], w_ref[...])


def _na_out_t_kernel(x_ref, ot_ref, mod_ref, w_ref, out_ref):
    out_ref[0] = x_ref[0] + _mod_part(mod_ref[0], 2) * _dot_tn(ot_ref[0], w_ref[...])


def _mixer_out(kernel_fn, name, x, act_specs, acts, consts, mods, cond_row, w_out):
    bsz, n, _ = x.shape
    rows = _block_rows(n)
    w = w_out.astype(BF16)
    specs = [pl.BlockSpec((1, a.shape[1], rows), lambda b, t: (b, 0, t)) if s == "t"
             else _row_spec(rows, a.shape[-1]) for s, a in zip(act_specs, acts)]
    return pl.pallas_call(
        kernel_fn,
        grid=(bsz, n // rows),
        in_specs=([_row_spec(rows, D_MODEL)] + specs + [_resident(c.shape) for c in consts]
                  + [_mod_spec(cond_row), _resident(w.shape)]),
        out_specs=_row_spec(rows, D_MODEL),
        out_shape=jax.ShapeDtypeStruct((bsz, n, D_MODEL), F32),
        compiler_params=_cparams(),
        name=name,
    )(x, *acts, *consts, mods, w)


def _mlstm_proj_kernel(xp_ref, xm_ref, xn_ref, mod_ref, gain_ref, wqk_ref, cw_ref, cb_ref, wvot_ref, wg_ref,
                       bg_ref, wgt_ref, bgt_ref, q_ref, k_ref, vt_ref, ot_ref, bc_ref, gr_ref, bm_ref, hb_ref,
                       *, rows):
    mod = mod_ref[0]
    shift, scale = _mod_part(mod, 0), _mod_part(mod, 1)
    _fill_halo_block(hb_ref, xp_ref, xm_ref, xn_ref, gain_ref[...], scale, shift, rows)
    hm = hb_ref[HALO:HALO + rows]
    L, H = A_CHUNK, A_HEADS
    gates_c = _dot(hm, wg_ref[...]) + bg_ref[...]
    gates_r = _dot_nt(wgt_ref[...], hm) + bgt_ref[...]
    key = lax.broadcasted_iota(jnp.int32, (L, L), 0)
    qry = lax.broadcasted_iota(jnp.int32, (L, L), 1)
    fwd_mask = jnp.where(qry <= key, 1.0, 0.0).astype(BF16)
    bwd_mask = jnp.where(qry >= key, 1.0, 0.0).astype(BF16)
    fwd_lane = lax.broadcasted_iota(jnp.int32, (L, GATE_PAD), 1) < H
    fwd_row = lax.broadcasted_iota(jnp.int32, (2 * H, L), 0) < H
    lane = lax.broadcasted_iota(jnp.int32, (2 * H, L), 1)
    spans = [slice(c * L, (c + 1) * L) for c in range(rows // L)]
    run_max = []
    for cs in spans:
        logf_c = _log_sigmoid(gates_c[cs, GATE_PAD:])
        g_c = jnp.where(fwd_lane, _cumsum_rows(fwd_mask, logf_c), _cumsum_rows(bwd_mask, logf_c))
        bc_ref[0, cs, :] = gates_c[cs, :GATE_PAD] - g_c
        logf_r = _log_sigmoid(gates_r[2 * H:, cs])
        g_r = jnp.where(fwd_row, _cumsum_cols(logf_r, fwd_mask), _cumsum_cols(logf_r, bwd_mask))
        gr_ref[0, :, cs] = g_r
        run_max.append(gates_r[:2 * H, cs] - g_r)
    dist = 1
    while dist < L:
        for c, rm in enumerate(run_max):
            seen_f = jnp.where(lane >= dist, pltpu.roll(rm, dist, 1), -jnp.inf)
            seen_b = jnp.where(lane < L - dist, pltpu.roll(rm, L - dist, 1), -jnp.inf)
            run_max[c] = jnp.maximum(rm, jnp.where(fwd_row, seen_f, seen_b))
        dist *= 2
    for cs, rm in zip(spans, run_max):
        bm_ref[0, :, cs] = rm

    qk = _silu(_dwconv3(_dot(hb_ref[...], wqk_ref[...]), cw_ref[...], cb_ref[...], rows))
    q_ref[0] = (qk[:, :A_QK] * (A_DK ** -0.5)).astype(BF16)
    k_ref[0] = qk[:, A_QK:].astype(BF16)
    vot = _dot_nt(wvot_ref[...], hm)
    vt_ref[0] = vot[:A_V].astype(BF16)
    ot_ref[0] = vot[A_V:].astype(BF16)


def _mlstm_weights(w_in, b_gate, conv_w, conv_b):
    hh = A_HEADS
    wqk = w_in[:, :2 * A_QK].astype(BF16)
    wvot = w_in[:, 2 * A_QK:2 * A_QK + 2 * A_V].T.astype(BF16)
    wgate = w_in[:, 2 * A_QK + 2 * A_V:]
    pad = GATE_PAD - 2 * hh

    def regroup(t, axis, pad):
        i_f, f_f, i_b, f_b = jnp.split(t, 4, axis=axis)
        widths = [(0, 0)] * (t.ndim - 1) + [(0, pad)]
        return jnp.concatenate([jnp.pad(jnp.concatenate([i_f, i_b], axis), widths),
                                jnp.pad(jnp.concatenate([f_f, f_b], axis), widths)], axis)

    wg = regroup(wgate, 1, pad).astype(BF16)
    bg = regroup(b_gate, 0, pad).reshape(1, 2 * GATE_PAD)
    wgt = regroup(wgate, 1, 0).T.astype(BF16)
    bgt = regroup(b_gate, 0, 0).reshape(4 * hh, 1)
    return wqk, conv_w, conv_b.reshape(1, -1), wvot, wg, bg, wgt, bgt


def _mlstm_proj(x, mods, cond_row, gain, weights):
    bsz, n, _ = x.shape
    rows = _block_rows(n)
    prev, main, nxt = _halo_specs(n, rows)
    t_spec = pl.BlockSpec((1, A_V, rows), lambda b, t: (b, 0, t))
    g_spec = pl.BlockSpec((1, 2 * A_HEADS, rows), lambda b, t: (b, 0, t))
    g_shape = jax.ShapeDtypeStruct((bsz, 2 * A_HEADS, n), F32)
    return pl.pallas_call(
        functools.partial(_mlstm_proj_kernel, rows=rows),
        grid=(bsz, n // rows),
        in_specs=[prev, main, nxt, _mod_spec(cond_row), _resident((1, D_MODEL))]
                 + [_resident(w.shape) for w in weights],
        out_specs=[_row_spec(rows, A_QK), _row_spec(rows, A_QK), t_spec, t_spec,
                   _row_spec(rows, GATE_PAD), g_spec, g_spec],
        out_shape=[jax.ShapeDtypeStruct((bsz, n, A_QK), BF16), jax.ShapeDtypeStruct((bsz, n, A_QK), BF16),
                   jax.ShapeDtypeStruct((bsz, A_V, n), BF16), jax.ShapeDtypeStruct((bsz, A_V, n), BF16),
                   jax.ShapeDtypeStruct((bsz, n, GATE_PAD), F32), g_shape, g_shape],
        scratch_shapes=[pltpu.VMEM((rows + 2 * HALO, D_MODEL), BF16)],
        compiler_params=_cparams(),
        name="mlstm_proj",
    )(x, x, x, mods, gain.reshape(1, D_MODEL), *weights)


def _both_directions(one_direction, n_in, n_out, n_scratch):
    def kernel(*refs, cps):
        cuts = np.cumsum([0, n_in, n_in, n_out, n_out, n_scratch, n_scratch])
        groups = [refs[a:b] for a, b in zip(cuts[:-1], cuts[1:])]
        fwd = groups[0] + groups[2] + groups[4]
        bwd = groups[1] + groups[3] + groups[5]
        for phase in ("init", "main", "final"):
            one_direction(*fwd, reverse=False, cps=cps, phase=phase)
            one_direction(*bwd, reverse=True, cps=cps, phase=phase)
    return kernel


def _mlstm_scan_direction(q_ref, k_ref, vt_ref, bc_ref, gr_ref, bm_ref, s0_ref, m0_ref,
                          ht_ref, s_out_ref, m_out_ref, state_ref, m_ref, *, reverse, cps, phase):
    L, H = A_CHUNK, A_HEADS
    step = pl.program_id(1)

    if phase == "init":
        @pl.when(step == 0)
        def _():
            state_ref[...] = s0_ref[0]
            m_ref[...] = m0_ref[0]
        return
    if phase == "final":
        @pl.when(step == pl.num_programs(1) - 1)
        def _():
            s_out_ref[0] = state_ref[...]
            m_out_ref[0] = m_ref[...]
        return

    key = lax.broadcasted_iota(jnp.int32, (L, L), 0)
    qry = lax.broadcasted_iota(jnp.int32, (L, L), 1)
    keep_t = (key >= qry) if reverse else (key <= qry)
    lane0 = H if reverse else 0
    end = 0 if reverse else L - 1

    order = range(cps - 1, -1, -1) if reverse else range(cps)
    span = lambda c: slice(c * L, (c + 1) * L)
    head_sub = lax.broadcasted_iota(jnp.int32, A_MSTATE, 0)
    head_lane = lax.broadcasted_iota(jnp.int32, A_MSTATE, 1)

    m_prev = m_ref[...]
    mx, inter, floor, decay, b_c, w_c = {}, {}, {}, {}, {}, {}
    for c in order:
        cum_r = gr_ref[0, lane0:lane0 + H, span(c)]
        run_max = bm_ref[0, lane0:lane0 + H, span(c)]
        mx[c] = jnp.maximum(run_max, m_prev)
        mx_end = jnp.broadcast_to(mx[c][:, end:end + 1], A_MSTATE)
        g_tot = jnp.broadcast_to(cum_r[:, end:end + 1], A_MSTATE)
        inter[c] = jnp.exp(m_prev - mx[c])
        floor[c] = jnp.exp(-(cum_r + mx[c]))
        decay[c] = jnp.exp(m_prev - mx_end)
        mx_end_lanes = jnp.sum(jnp.where(head_lane == head_sub + lane0, mx_end, 0.0), axis=0, keepdims=True)
        b_c[c] = bc_ref[0, span(c), :]
        w_c[c] = jnp.exp(b_c[c] - mx_end_lanes)
        m_prev = g_tot + mx_end
    m_ref[...] = m_prev

    items = [(c, h) for c in range(cps) for h in range(H)]
    pairs = [(c, p) for c in range(cps) for p in range(H // 2)]
    ones = jnp.ones((A_ONES, L), BF16)
    low = lax.broadcasted_iota(jnp.int32, (L, 2 * A_DK), 1) < A_DK
    low_row = low[0:1]
    k2 = {(c, p): k_ref[0, span(c), 2 * p * A_DK:2 * (p + 1) * A_DK] for c, p in pairs}
    q2 = {(c, p): q_ref[0, span(c), 2 * p * A_DK:2 * (p + 1) * A_DK] for c, p in pairs}
    q_bd = {i: jnp.concatenate([jnp.where(low, q2[i], 0), jnp.where(low, 0, q2[i])], axis=0).astype(BF16)
            for i in pairs}
    vaug_t = {(c, h): jnp.concatenate([vt_ref[0, h * A_DV:(h + 1) * A_DV, span(c)], ones], axis=0)
              for c, h in items}
    s_pair = {i: _dot_nt(k2[i], q_bd[i]) for i in pairs}
    state_add = {}
    for c, p in pairs:
        col = lambda h: w_c[c][:, lane0 + h:lane0 + h + 1]
        kw = (k2[c, p].astype(F32) * jnp.where(low, col(2 * p), col(2 * p + 1))).astype(BF16)
        state_add[c, p] = (_dot(vaug_t[c, 2 * p], jnp.where(low, kw, 0))
                           + _dot(vaug_t[c, 2 * p + 1], jnp.where(low, 0, kw)))
    p_t = {}
    for c, h in items:
        b_col = b_c[c][:, lane0 + h:lane0 + h + 1]
        w_t = jnp.exp(jnp.where(keep_t, b_col - mx[c][h:h + 1, :], -jnp.inf))
        p_t[c, h] = (s_pair[c, h // 2][:, (h % 2) * L:(h % 2 + 1) * L] * w_t).astype(BF16)
    local_t = {i: _dot(vaug_t[i], p_t[i]) for i in items}

    for p in range(H // 2):
        state = state_ref[p]
        for c in order:
            carry = _dot_nt(state.astype(BF16), q_bd[c, p])
            for h in (2 * p, 2 * p + 1):
                tot = local_t[c, h] + inter[c][h:h + 1, :] * carry[:, (h % 2) * L:(h % 2 + 1) * L]
                den = jnp.maximum(jnp.abs(tot[A_DV:A_DV + 1, :]), floor[c][h:h + 1, :])
                ht_ref[0, h * A_DV:(h + 1) * A_DV, span(c)] = (tot[:A_DV] / den).astype(BF16)
            scale = jnp.where(low_row, decay[c][2 * p:2 * p + 1, :], decay[c][2 * p + 1:2 * p + 2, :])
            state = scale * state + state_add[c, p]
        state_ref[p] = state


def _chunk_order(n_chunks, reverse):
    return (lambda c: n_chunks - 1 - c) if reverse else (lambda c: c)


def _mlstm_scan(q, k, vt, b_c, g_r, bmax_r, carry_f, carry_b):
    bsz, n, _ = q.shape
    cps = min(A_CHUNKS_PER_STEP, n // A_CHUNK)
    L = A_CHUNK * cps
    s_spec = pl.BlockSpec((1,) + A_STATE, lambda b, c: (b, 0, 0, 0))
    m_spec = pl.BlockSpec((1,) + A_MSTATE, lambda b, c: (b, 0, 0))

    def specs(reverse):
        cidx = _chunk_order(n // L, reverse)
        tok = lambda w: pl.BlockSpec((1, L, w), lambda b, c: (b, cidx(c), 0))
        chan = lambda r: pl.BlockSpec((1, r, L), lambda b, c: (b, 0, cidx(c)))
        return ([tok(A_QK), tok(A_QK), chan(A_V), tok(GATE_PAD), chan(2 * A_HEADS), chan(2 * A_HEADS),
                 s_spec, m_spec], [chan(A_V), s_spec, m_spec])

    (in_f, out_f), (in_b, out_b) = specs(False), specs(True)
    shapes = [jax.ShapeDtypeStruct((bsz, A_V, n), BF16), jax.ShapeDtypeStruct((bsz,) + A_STATE, F32),
              jax.ShapeDtypeStruct((bsz,) + A_MSTATE, F32)]
    scratch = [pltpu.VMEM(A_STATE, F32), pltpu.VMEM(A_MSTATE, F32)]
    hf, sf, mf, hb, sb, mb = pl.pallas_call(
        functools.partial(_both_directions(_mlstm_scan_direction, 8, 3, 2), cps=cps),
        grid=(bsz, n // L),
        in_specs=in_f + in_b,
        out_specs=out_f + out_b,
        out_shape=shapes + shapes,
        scratch_shapes=scratch + scratch,
        compiler_params=_cparams(),
        name="mlstm_scan",
    )(q, k, vt, b_c, g_r, bmax_r, *carry_f, q, k, vt, b_c, g_r, bmax_r, *carry_b)
    return hf, hb, (sf, mf), (sb, mb)


def _mlstm_mixer(x_ctx, x_lat, mods, ctx_row, gain, w_in, b_gate, conv_w, conv_b):
    weights = _mlstm_weights(w_in, b_gate, conv_w, conv_b)
    bsz = x_lat.shape[0]
    zero = (jnp.zeros((bsz,) + A_STATE, F32), jnp.zeros((bsz,) + A_MSTATE, F32))
    carry_f, carry_b = zero, zero
    acts = []
    for x, row in ((x_ctx, ctx_row), (x_lat, None)):
        q, k, vt, ot, b_c, g_r, bmax_r = _mlstm_proj(x, mods, row, gain, weights)
        hf, hb, carry_f, carry_b = _mlstm_scan(q, k, vt, b_c, g_r, bmax_r, carry_f, carry_b)
        acts.append((hf, hb, ot))
    return acts[0], acts[1]


def _gla_proj_kernel(x_ref, mod_ref, gain_ref, wqkv_ref, wr_ref, wa_ref, wa2_ref, ba_ref,
                     q_ref, k_ref, v_ref, r_ref, gf_ref, gb_ref, *, rows):
    mod = mod_ref[0]
    h = _norm_mod(x_ref[0], gain_ref[...], _mod_part(mod, 1), _mod_part(mod, 0)).astype(BF16)
    L = B_CHUNK
    a = _dot(h, wa_ref[...]).astype(BF16)
    row = lax.broadcasted_iota(jnp.int32, (L, L), 0)
    col = lax.broadcasted_iota(jnp.int32, (L, L), 1)
    for d, out_ref in enumerate((gf_ref, gb_ref)):
        mask = jnp.where((col >= row) if d else (col <= row), 1.0, 0.0).astype(BF16)
        g = _log_sigmoid(_dot(a, wa2_ref[d]) + ba_ref[d]) / B_TAU
        for c in range(rows // L):
            out_ref[0, c * L:(c + 1) * L, :] = _cumsum_rows(mask, g[c * L:(c + 1) * L])
    qkv = _dot(h, wqkv_ref[...])
    q_ref[0] = qkv[:, :B_QK].astype(BF16)
    k_ref[0] = qkv[:, B_QK:2 * B_QK].astype(BF16)
    v_ref[0] = qkv[:, 2 * B_QK:].astype(BF16)
    r_ref[0] = _dot(h, wr_ref[...]).astype(BF16)


def _gla_proj(x, mods, cond_row, gain, weights):
    bsz, n, _ = x.shape
    rows = _block_rows(n)
    return pl.pallas_call(
        functools.partial(_gla_proj_kernel, rows=rows),
        grid=(bsz, n // rows),
        in_specs=[_row_spec(rows, D_MODEL), _mod_spec(cond_row), _resident((1, D_MODEL))]
                 + [_resident(w.shape) for w in weights],
        out_specs=[_row_spec(rows, B_QK), _row_spec(rows, B_QK), _row_spec(rows, B_V), _row_spec(rows, B_V),
                   _row_spec(rows, B_QK), _row_spec(rows, B_QK)],
        out_shape=[jax.ShapeDtypeStruct((bsz, n, B_QK), BF16), jax.ShapeDtypeStruct((bsz, n, B_QK), BF16),
                   jax.ShapeDtypeStruct((bsz, n, B_V), BF16), jax.ShapeDtypeStruct((bsz, n, B_V), BF16),
                   jax.ShapeDtypeStruct((bsz, n, B_QK), F32), jax.ShapeDtypeStruct((bsz, n, B_QK), F32)],
        compiler_params=_cparams(),
        name="gla_proj",
    )(x, mods, gain.reshape(1, D_MODEL), *weights)


def _gla_scan_direction(q_ref, k_ref, v_ref, g_ref, s0_ref, o_ref, s_out_ref, state_ref, att_ref,
                        *, reverse, cps, phase):
    L, sub = B_CHUNK, B_SUB
    nsub = L // sub
    step = pl.program_id(1)

    if phase == "init":
        @pl.when(step == 0)
        def _():
            state_ref[...] = s0_ref[0]
        return
    if phase == "final":
        @pl.when(step == pl.num_programs(1) - 1)
        def _():
            s_out_ref[0] = state_ref[...]
        return

    items = [(c, h) for c in range(cps) for h in range(B_HEADS)]
    rows = lambda c: slice(c * L, (c + 1) * L)
    Gs = {(c, h): g_ref[0, rows(c), h * B_DK:(h + 1) * B_DK] for c, h in items}
    qs = {(c, h): q_ref[0, rows(c), h * B_DK:(h + 1) * B_DK].astype(F32) * (B_DK ** -0.5) for c, h in items}
    ks = {(c, h): k_ref[0, rows(c), h * B_DK:(h + 1) * B_DK].astype(F32) for c, h in items}
    vs = {(c, h): v_ref[0, rows(c), h * B_DV:(h + 1) * B_DV] for c, h in items}
    g_tot = {i: (G[0:1] if reverse else G[L - 1:L]) for i, G in Gs.items()}
    q_carry = {i: (qs[i] * jnp.exp(Gs[i])).astype(BF16) for i in items}
    state_add = {i: _dot_tn(vs[i], (ks[i] * jnp.exp(g_tot[i] - Gs[i])).astype(BF16)) for i in items}

    def edge_scores(G, qh, kh, own_block):
        def edge(b):
            first = (b * sub) if own_block != reverse else (b * sub + sub - 1)
            return G[first:first + 1]
        edges = jnp.concatenate([jnp.broadcast_to(edge(b), (sub, B_DK)) for b in range(nsub)], axis=0)
        k_edge = kh * jnp.exp(edges - G)
        q_cols, k_cols = [], []
        for b in range(nsub):
            lo, hi = b * sub, (b + 1) * sub
            if reverse:
                top = lo + sub if own_block else lo
                seen = [qh[:top] * jnp.exp(G[:top] - edge(b)), jnp.zeros((L - top, B_DK), F32)]
            else:
                bot = lo if own_block else hi
                seen = [jnp.zeros((bot, B_DK), F32), qh[bot:] * jnp.exp(G[bot:] - edge(b))]
            q_cols.append(jnp.concatenate([s for s in seen if s.shape[0]], axis=0).astype(BF16))
            k_parts = [jnp.zeros((lo, B_DK), F32), k_edge[lo:hi], jnp.zeros((L - hi, B_DK), F32)]
            k_cols.append(jnp.concatenate([s for s in k_parts if s.shape[0]], axis=0).astype(BF16))
        return _dot_nt(jnp.concatenate(q_cols, axis=1), jnp.concatenate(k_cols, axis=1))

    mild = jnp.min(g_ref[0]) >= -B_MILD_DECAY

    @pl.when(mild)
    def _():
        row = lax.broadcasted_iota(jnp.int32, (L, L), 0)
        col = lax.broadcasted_iota(jnp.int32, (L, L), 1)
        keep = (col >= row) if reverse else (col <= row)
        for n, i in enumerate(items):
            att_ref[n] = jnp.where(keep, edge_scores(Gs[i], qs[i], ks[i], True), 0.0)

    @pl.when(jnp.logical_not(mild))
    def _():
        pair = lax.broadcasted_iota(jnp.int32, (L * sub, 128), 0)
        lane = lax.broadcasted_iota(jnp.int32, (L * sub, 128), 1)
        place = lane == (pair // sub // sub) * sub + pair % sub
        sel = (lax.broadcasted_iota(jnp.int32, (L, L * sub), 1) // sub
               == lax.broadcasted_iota(jnp.int32, (L, L * sub), 0))
        sel_bf16 = jnp.where(sel, 1.0, 0.0).astype(BF16)
        ones_red = jnp.ones((B_DK, 128), BF16)
        srow = lax.broadcasted_iota(jnp.int32, (sub, B_DK), 0)

        def pair_sums(G, qh, kh):
            pieces = []
            for b in range(nsub):
                Gb = G[b * sub:(b + 1) * sub]
                kb = kh[b * sub:(b + 1) * sub]
                for j in range(sub):
                    r = b * sub + j
                    vis = (srow >= j) if reverse else (srow <= j)
                    pieces.append(qh[r:r + 1] * kb * jnp.exp(jnp.where(vis, G[r:r + 1] - Gb, -jnp.inf)))
            return _dot(jnp.concatenate(pieces, axis=0).astype(BF16), ones_red)

        att_cross = {i: edge_scores(Gs[i], qs[i], ks[i], False) for i in items}
        sums = {i: pair_sums(Gs[i], qs[i], ks[i]) for i in items}
        for n, i in enumerate(items):
            local = _dot(sel_bf16, jnp.where(place, sums[i], 0.0).astype(BF16))
            att_ref[n] = att_cross[i] + local[:, :L]

    o_local = {i: _dot(att_ref[n].astype(BF16), vs[i]) for n, i in enumerate(items)}

    order = range(cps - 1, -1, -1) if reverse else range(cps)
    for h in range(B_HEADS):
        state = state_ref[h]
        for c in order:
            o_ref[0, rows(c), h * B_DV:(h + 1) * B_DV] = (
                o_local[c, h] + _dot_nt(q_carry[c, h], state.astype(BF16))).astype(BF16)
            state = state * jnp.exp(g_tot[c, h]) + state_add[c, h]
        state_ref[h] = state


def _gla_scan(q, k, v, g_f, g_b, state_f, state_b):
    bsz, n, _ = q.shape
    cps = min(B_CHUNKS_PER_STEP, n // B_CHUNK)
    L = B_CHUNK * cps
    sshape = (B_HEADS, B_DV, B_DK)
    s_spec = pl.BlockSpec((1,) + sshape, lambda b, c: (b, 0, 0, 0))

    def specs(reverse):
        cidx = _chunk_order(n // L, reverse)
        tok = lambda w: pl.BlockSpec((1, L, w), lambda b, c: (b, cidx(c), 0))
        return [tok(B_QK), tok(B_QK), tok(B_V), tok(B_QK), s_spec], [tok(B_V), s_spec]

    (in_f, out_f), (in_b, out_b) = specs(False), specs(True)
    shapes = [jax.ShapeDtypeStruct((bsz, n, B_V), BF16), jax.ShapeDtypeStruct((bsz,) + sshape, F32)]
    of, sf, ob, sb = pl.pallas_call(
        functools.partial(_both_directions(_gla_scan_direction, 5, 2, 2), cps=cps),
        grid=(bsz, n // L),
        in_specs=in_f + in_b,
        out_specs=out_f + out_b,
        out_shape=shapes + shapes,
        scratch_shapes=[pltpu.VMEM(sshape, F32), pltpu.VMEM((cps * B_HEADS, B_CHUNK, B_CHUNK), F32)] * 2,
        compiler_params=_cparams(),
        name="gla_scan",
    )(q, k, v, g_f, state_f, q, k, v, g_b, state_b)
    return of, ob, sf, sb


def _gla_mixer(x_ctx, x_lat, mods, ctx_row, gain, w_in, w_a2, b_a):
    wqkv = w_in[:, :2 * B_QK + B_V].astype(BF16)
    wr = w_in[:, 2 * B_QK + B_V:2 * B_QK + 2 * B_V].astype(BF16)
    wa = jnp.pad(w_in[:, 2 * B_QK + 2 * B_V:], ((0, 0), (0, GATE_PAD - 2 * B_RANK))).astype(BF16)
    wa2 = jnp.stack([jnp.zeros((GATE_PAD, B_QK), F32).at[d * B_RANK:(d + 1) * B_RANK].set(w_a2[d])
                     for d in range(2)]).astype(BF16)
    ba = b_a.reshape(2, 1, B_QK)
    bsz = x_lat.shape[0]
    state_f = state_b = jnp.zeros((bsz, B_HEADS, B_DV, B_DK), F32)
    acts = []
    for x, row in ((x_ctx, ctx_row), (x_lat, None)):
        q, k, v, r, g_f, g_b = _gla_proj(x, mods, row, gain, (wqkv, wr, wa, wa2, ba))
        of, ob, state_f, state_b = _gla_scan(q, k, v, g_f, g_b, state_f, state_b)
        acts.append((of, ob, r))
    return acts[0], acts[1]


def _na_proj_kernel_ctx(x_ref, mod_ref, gain_ref, w_ref, wvt_ref, q_ref, k_ref, v_ref, vt_ref):
    mod = mod_ref[0]
    h = _norm_mod(x_ref[0], gain_ref[...], _mod_part(mod, 1), _mod_part(mod, 0)).astype(BF16)
    qkv = _dot(h, w_ref[...])
    q_ref[0] = (qkv[:, :D_MODEL] * (C_DH ** -0.5)).astype(BF16)
    k_ref[0] = qkv[:, D_MODEL:2 * D_MODEL].astype(BF16)
    v_ref[0] = qkv[:, 2 * D_MODEL:].astype(BF16)
    vt_ref[0] = _dot_nt(wvt_ref[...], h).astype(BF16)


def _na_proj_kernel_lat(x_ref, mod_ref, gain_ref, w_ref, wvt_ref, q_ref, k_ref, vt_ref):
    mod = mod_ref[0]
    h = _norm_mod(x_ref[0], gain_ref[...], _mod_part(mod, 1), _mod_part(mod, 0)).astype(BF16)
    qk = _dot(h, w_ref[...])
    q_ref[0] = (qk[:, :D_MODEL] * (C_DH ** -0.5)).astype(BF16)
    k_ref[0] = qk[:, D_MODEL:].astype(BF16)
    vt_ref[0] = _dot_nt(wvt_ref[...], h).astype(BF16)


def _na_proj(x, mods, cond_row, gain, w_in, *, with_v):
    bsz, n, _ = x.shape
    rows = _block_rows(n)
    wvt = w_in[:, 2 * D_MODEL:].T.astype(BF16)
    w = (w_in if with_v else w_in[:, :2 * D_MODEL]).astype(BF16)
    tok = jax.ShapeDtypeStruct((bsz, n, D_MODEL), BF16)
    vt_shape = jax.ShapeDtypeStruct((bsz, D_MODEL, n), BF16)
    vt_spec = pl.BlockSpec((1, D_MODEL, rows), lambda b, t: (b, 0, t))
    rs = _row_spec(rows, D_MODEL)
    return pl.pallas_call(
        _na_proj_kernel_ctx if with_v else _na_proj_kernel_lat,
        grid=(bsz, n // rows),
        in_specs=[rs, _mod_spec(cond_row), _resident((1, D_MODEL)), _resident(w.shape), _resident(wvt.shape)],
        out_specs=[rs, rs, rs, vt_spec] if with_v else [rs, rs, vt_spec],
        out_shape=[tok, tok, tok, vt_shape] if with_v else [tok, tok, vt_shape],
        compiler_params=_cparams(),
        name="na_proj_ctx" if with_v else "na_proj_lat",
    )(x, mods, gain.reshape(1, D_MODEL), w, wvt)


def _na_bias_table(rpb):
    colq = np.arange(GRID_W)[None, :]
    colk = np.arange(GRID_W)[:, None]
    start = np.clip(colq - C_WIN_COLS // 2, 0, GRID_W - C_WIN_COLS)
    valid = (colk >= start) & (colk < start + C_WIN_COLS)
    w2 = 2 * GRID_W
    lead = GRID_W - C_WIN_COLS
    padded = jnp.pad(rpb, ((0, 0), (0, 0), (lead, w2 - lead - rpb.shape[-1])))
    skew = jnp.tile(padded, (1, 1, GRID_W))[..., :GRID_W * (w2 - 1)]
    skew = skew.reshape(rpb.shape[0], rpb.shape[1], GRID_W, w2 - 1)[..., GRID_W - 1:]
    bias = jnp.where(valid[None, None], jnp.swapaxes(skew, 2, 3), -jnp.inf)
    bias = bias.transpose(1, 2, 0, 3).reshape(2 * C_WIN_ROWS - 1, GRID_W, C_HEADS * GRID_W)
    return jnp.concatenate([jnp.full((1,) + bias.shape[1:], -jnp.inf, F32), bias], axis=0)


def _na_lat_kernel(q_ref, k_ref, kc_ref, vt_ref, vtc_ref, tab_ref, ot_ref, *, rows):
    r0 = 2 * pl.program_id(1)
    gw = C_GROUP * C_DH
    nk = C_KROWS * GRID_W
    first = lambda r: jnp.clip(r - C_WIN_ROWS // 2, 0, rows - C_WIN_ROWS)
    rs_load = jnp.minimum(first(r0) - first(r0) % 2, rows - C_KROWS)
    start = pl.multiple_of(rs_load * GRID_W, 2 * GRID_W)
    entries = []
    for i in range(C_KROWS):
        kr = rs_load + i
        per_q = []
        for r in (r0, r0 + 1):
            ok = jnp.logical_and(kr >= first(r), kr < first(r) + C_WIN_ROWS)
            per_q.append(jnp.where(ok, kr - r + C_WIN_ROWS, 0))
        entries.append(per_q)

    blk_r = lax.broadcasted_iota(jnp.int32, (gw, gw), 0) // C_DH
    blk_c = lax.broadcasted_iota(jnp.int32, (gw, gw), 1) // C_DH
    same_head = blk_r == blk_c

    def block_diag(qg):
        return jnp.where(same_head, jnp.concatenate([qg] * C_GROUP, axis=0), 0).astype(BF16)

    groups = range(C_HEADS // C_GROUP)
    gsl = [slice(g * gw, (g + 1) * gw) for g in groups]
    q_bd = [jnp.concatenate([block_diag(q_ref[0, :GRID_W, gs]), block_diag(q_ref[0, GRID_W:, gs])], axis=0)
            for gs in gsl]
    s_loc = [_dot_nt(k_ref[0, pl.ds(start, nk), gs], qb) for gs, qb in zip(gsl, q_bd)]
    s_ctx = [_dot_nt(kc_ref[0, :, gs], qb) for gs, qb in zip(gsl, q_bd)]
    p_loc, p_ctx, denom = [], [], []
    for g in groups:
        bias = jnp.concatenate(
            [jnp.concatenate([tab_ref[e, :, gsl[g]] for e in per_q], axis=1) for per_q in entries], axis=0)
        sl = s_loc[g] + bias
        m = jnp.maximum(jnp.max(sl, axis=0, keepdims=True), jnp.max(s_ctx[g], axis=0, keepdims=True))
        pl_, pc_ = jnp.exp(sl - m), jnp.exp(s_ctx[g] - m)
        denom.append(jnp.sum(pl_, axis=0, keepdims=True) + jnp.sum(pc_, axis=0, keepdims=True))
        p_loc.append(pl_.astype(BF16))
        p_ctx.append(pc_.astype(BF16))
    for g in groups:
        ot = (_dot(vt_ref[0, gsl[g], pl.ds(start, nk)], p_loc[g])
              + _dot(vtc_ref[0, gsl[g], :], p_ctx[g])) / denom[g]
        for qr in range(2):
            for h in range(C_GROUP):
                ch = slice(g * gw + h * C_DH, g * gw + (h + 1) * C_DH)
                ot_ref[0, ch, qr * GRID_W:(qr + 1) * GRID_W] = (
                    ot[h * C_DH:(h + 1) * C_DH, qr * gw + h * C_DH:qr * gw + (h + 1) * C_DH].astype(BF16))


def _na_lat_attention(q, k, kc, vt, vtc, tab):
    bsz, n, _ = q.shape
    rows = n // GRID_W
    n_ctx = kc.shape[1]
    assert rows >= C_KROWS and rows % 2 == 0
    return pl.pallas_call(
        functools.partial(_na_lat_kernel, rows=rows),
        grid=(bsz, rows // 2),
        in_specs=[pl.BlockSpec((1, 2 * GRID_W, D_MODEL), lambda b, r: (b, r, 0)),
                  pl.BlockSpec((1, n, D_MODEL), lambda b, r: (b, 0, 0)),
                  pl.BlockSpec((1, n_ctx, D_MODEL), lambda b, r: (b, 0, 0)),
                  pl.BlockSpec((1, D_MODEL, n), lambda b, r: (b, 0, 0)),
                  pl.BlockSpec((1, D_MODEL, n_ctx), lambda b, r: (b, 0, 0)),
                  _resident(tab.shape)],
        out_specs=pl.BlockSpec((1, D_MODEL, 2 * GRID_W), lambda b, r: (b, 0, r)),
        out_shape=jax.ShapeDtypeStruct((bsz, D_MODEL, n), BF16),
        compiler_params=_cparams(),
        name="na_lat_attention",
    )(q, k, kc, vt, vtc, tab)


def _na_ctx_kernel(q_ref, k_ref, v_ref, o_ref):
    hsl = [slice(h * C_DH, (h + 1) * C_DH) for h in range(C_HEADS)]
    s = [_dot_nt(q_ref[0, :, hs], k_ref[0, :, hs]) for hs in hsl]
    p = [jnp.exp(sh - jnp.max(sh, axis=-1, keepdims=True)) for sh in s]
    o = [_dot(ph.astype(BF16), v_ref[0, :, hs]) for ph, hs in zip(p, hsl)]
    for oh, ph, hs in zip(o, p, hsl):
        o_ref[0, :, hs] = (oh / jnp.sum(ph, axis=-1, keepdims=True)).astype(BF16)


def _na_ctx_attention(q, k, v):
    bsz, n, _ = q.shape
    spec = pl.BlockSpec((1, n, D_MODEL), lambda b: (b, 0, 0))
    return pl.pallas_call(
        _na_ctx_kernel,
        grid=(bsz,),
        in_specs=[spec, spec, spec],
        out_specs=spec,
        out_shape=jax.ShapeDtypeStruct((bsz, n, D_MODEL), BF16),
        compiler_params=_cparams(1),
        name="na_ctx_attention",
    )(q, k, v)


def kernel(x, c, ctx, c_ctx, w_ada, b_ada, norm_mix, norm_ffn, w_up, ffn_conv_w, ffn_conv_b, w_down,
           a_w_in, a_b_gate, a_conv_w, a_conv_b, a_w_out,
           b_w_in, b_w_a2, b_b_a, b_norm, b_w_out,
           c_w_in, c_rpb, c_w_out, norm_final):
    bsz = x.shape[0]
    depth = w_ada.shape[0]
    ctx_row = bsz
    assert bsz < COND_ROWS

    cond = jnp.concatenate([c, c_ctx[None], jnp.zeros((COND_ROWS - bsz - 1, D_MODEL), F32)], axis=0)
    mod_all = _ada_modulation(cond, w_ada, b_ada)
    x_ctx, x_lat = ctx, x

    for i in range(depth):
        last = i == depth - 1
        mods = mod_all[i].reshape(COND_ROWS, 1, 6 * D_MODEL)
        kind, j = i % 3, i // 3
        if kind == 0:
            acts_c, acts_l = _mlstm_mixer(x_ctx, x_lat, mods, ctx_row, norm_mix[i], a_w_in[j], a_b_gate[j],
                                          a_conv_w[j], a_conv_b[j])
            out = functools.partial(_mixer_out, _mlstm_out_kernel, "mlstm_out", consts=[], mods=mods,
                                    w_out=a_w_out[j], act_specs="ttt")
        elif kind == 1:
            acts_c, acts_l = _gla_mixer(x_ctx, x_lat, mods, ctx_row, norm_mix[i], b_w_in[j], b_w_a2[j], b_b_a[j])
            out = functools.partial(_mixer_out, _gla_out_kernel, "gla_out", consts=[b_norm[j].reshape(1, B_DV)],
                                    mods=mods, w_out=b_w_out[j], act_specs="rrr")
        else:
            qc, kc, vc, vtc = _na_proj(x_ctx, mods, ctx_row, norm_mix[i], c_w_in[j], with_v=True)
            ql, kl, vtl = _na_proj(x_lat, mods, None, norm_mix[i], c_w_in[j], with_v=False)
            ot = _na_lat_attention(ql, kl, kc, vtl, vtc, _na_bias_table(c_rpb[j]))
            mid_lat = _mixer_out(_na_out_t_kernel, "na_out_lat", x_lat, "t", [ot], [], mods, None, c_w_out[j])
            if not last:
                oc = _na_ctx_attention(qc, kc, vc)
                mid_ctx = _mixer_out(_na_out_kernel, "na_out_ctx", x_ctx, "r", [oc], [], mods, ctx_row, c_w_out[j])
        if kind != 2:
            mid_lat = out(x=x_lat, acts=list(acts_l), cond_row=None)
            if not last:
                mid_ctx = out(x=x_ctx, acts=list(acts_c), cond_row=ctx_row)
        ffn_w = _ffn_weights(w_up[i], ffn_conv_w[i], ffn_conv_b[i], w_down[i])
        x_lat = _conv_ffn(mid_lat, mods, None, norm_ffn[i], ffn_w, norm_final, final=last)
        if not last:
            x_ctx = _conv_ffn(mid_ctx, mods, ctx_row, norm_ffn[i], ffn_w, norm_final, final=False)
    return x_lat
```

```python
import functools

import jax
import jax.numpy as jnp
import numpy as np
from jax import lax
from jax.experimental import pallas as pl
from jax.experimental.pallas import tpu as pltpu

F32 = jnp.float32
BF16 = jnp.bfloat16

D_MODEL = 1024
EPS = 1e-6
GRID_W = 64

HALO = 16
VMEM_LIMIT = 56 * 1024 * 1024
COND_ROWS = 16

A_HEADS, A_DK, A_DV, A_CHUNK = 8, 64, 128, 128
A_QK, A_V = A_HEADS * A_DK, A_HEADS * A_DV
A_ONES = 16
A_CHUNKS_PER_STEP = 4
A_STATE = (A_HEADS // 2, A_DV + A_ONES, 2 * A_DK)
A_MSTATE = (A_HEADS, 128)
B_HEADS, B_DK, B_DV, B_CHUNK = 4, 128, 256, 64
B_QK, B_V, B_RANK, B_TAU = B_HEADS * B_DK, B_HEADS * B_DV, 16, 16.0
B_SUB = 8
B_CHUNKS_PER_STEP = 4
B_MILD_DECAY = 60.0
C_HEADS, C_DH, C_WIN_ROWS, C_WIN_COLS = 16, 64, 8, 16
C_GROUP = 2
C_WAVE = 4
LOG2E = 1.4426950408889634
C_QSCALE = C_DH ** -0.5 * LOG2E
C_KROWS = C_WIN_ROWS + 2
FFN_DIM = 2816
FFN_TILE = 256
FFN_ROWS = 1024
GATE_PAD = 128

NT_DIMS = (((1,), (1,)), ((), ()))
TN_DIMS = (((0,), (0,)), ((), ()))


def _cparams(n_axes=2):
    return pltpu.CompilerParams(dimension_semantics=("arbitrary",) * n_axes, vmem_limit_bytes=VMEM_LIMIT)


def _resident(shape):
    nd = len(shape)
    return pl.BlockSpec(shape, lambda *_: (0,) * nd, pipeline_mode=pl.Buffered(1))


def _block_rows(n_tokens, largest=512):
    return max(r for r in (256, 512, 1024) if r <= largest and (n_tokens % r == 0 or r == 256))


def _dot(a, b):
    return jnp.dot(a, b, preferred_element_type=F32)


def _dot_nt(a, b):
    return lax.dot_general(a, b, NT_DIMS, preferred_element_type=F32)


def _dot_tn(a, b):
    return lax.dot_general(a, b, TN_DIMS, preferred_element_type=F32)


def _split3(x):
    x1 = x.astype(BF16)
    r1 = x - x1.astype(F32)
    x2 = r1.astype(BF16)
    x3 = (r1 - x2.astype(F32)).astype(BF16)
    return x1, x2, x3


def _cumsum_rows(mask_bf16, x):
    x1, x2, x3 = _split3(x)
    return _dot(mask_bf16, x1) + _dot(mask_bf16, x2) + _dot(mask_bf16, x3)


def _cumsum_cols(x, mask_bf16):
    x1, x2, x3 = _split3(x)
    return _dot_nt(x1, mask_bf16) + _dot_nt(x2, mask_bf16) + _dot_nt(x3, mask_bf16)


def _log_sigmoid(z):
    return jnp.minimum(z, 0.0) - jnp.log1p(jnp.exp(-jnp.abs(z)))


def _silu(z):
    return z * jax.nn.sigmoid(z)


def _rms(x, gain):
    return x * lax.rsqrt(jnp.mean(x * x, axis=-1, keepdims=True) + EPS) * gain


def _norm_mod(x, gain, scale, shift):
    return _rms(x, gain) * (1.0 + scale) + shift


def _mod_part(mod, i):
    return mod[:, i * D_MODEL:(i + 1) * D_MODEL]


def _dwconv3(u, w, b, rows):
    n = rows + 2 * HALO
    prev = pltpu.roll(u, 1, 0)[HALO:HALO + rows]
    nxt = pltpu.roll(u, n - 1, 0)[HALO:HALO + rows]
    return prev * w[0:1] + u[HALO:HALO + rows] * w[1:2] + nxt * w[2:3] + b


def _fill_halo_block(hb_ref, xp_ref, xm_ref, xn_ref, gain, scale, shift, rows):
    t = pl.program_id(1)
    has_prev = (t != 0).astype(F32)
    has_next = (t != pl.num_programs(1) - 1).astype(F32)
    hb_ref[0:HALO] = (_norm_mod(xp_ref[0], gain, scale, shift) * has_prev).astype(BF16)
    hb_ref[HALO:HALO + rows] = _norm_mod(xm_ref[0], gain, scale, shift).astype(BF16)
    hb_ref[HALO + rows:] = (_norm_mod(xn_ref[0], gain, scale, shift) * has_next).astype(BF16)


def _halo_specs(n_tokens, rows):
    per = rows // HALO
    last = n_tokens // HALO - 1
    prev = pl.BlockSpec((1, HALO, D_MODEL), lambda b, t: (b, jnp.maximum(t * per - 1, 0), 0))
    main = pl.BlockSpec((1, rows, D_MODEL), lambda b, t: (b, t, 0))
    nxt = pl.BlockSpec((1, HALO, D_MODEL), lambda b, t: (b, jnp.minimum((t + 1) * per, last), 0))
    return prev, main, nxt


def _mod_spec(cond_row):
    if cond_row is None:
        return pl.BlockSpec((1, 1, 6 * D_MODEL), lambda b, t: (b, 0, 0))
    return pl.BlockSpec((1, 1, 6 * D_MODEL), lambda b, t: (cond_row, 0, 0))


def _row_spec(rows, width):
    return pl.BlockSpec((1, rows, width), lambda b, t: (b, t, 0))


def _ada_kernel(c_ref, w_ref, b_ref, o_ref):
    s = _silu(c_ref[...]).astype(BF16)
    o_ref[0] = _dot(s, w_ref[0].astype(BF16)) + b_ref[0]


def _ada_modulation(cond, w_ada, b_ada):
    depth, _, n = w_ada.shape
    tn = 1024
    return pl.pallas_call(
        _ada_kernel,
        grid=(depth, n // tn),
        in_specs=[pl.BlockSpec((COND_ROWS, D_MODEL), lambda l, j: (0, 0)),
                  pl.BlockSpec((1, D_MODEL, tn), lambda l, j: (l, 0, j)),
                  pl.BlockSpec((1, 1, tn), lambda l, j: (l, 0, j))],
        out_specs=pl.BlockSpec((1, COND_ROWS, tn), lambda l, j: (l, 0, j)),
        out_shape=jax.ShapeDtypeStruct((depth, COND_ROWS, n), F32),
        compiler_params=_cparams(),
        name="ada_modulation",
    )(cond, w_ada, b_ada.reshape(depth, 1, n))


def _ffn_kernel(xp_ref, xm_ref, xn_ref, mod_ref, gain_ref, wup_ref, cw_ref, cb_ref, wdn_ref, gfin_ref,
                out_ref, hb_ref, z_ref, *, rows, final):
    mod = mod_ref[0]
    shift, scale, gate = _mod_part(mod, 3), _mod_part(mod, 4), _mod_part(mod, 5)
    _fill_halo_block(hb_ref, xp_ref, xm_ref, xn_ref, gain_ref[...], scale, shift, rows)
    for c in range(FFN_DIM // FFN_TILE):
        ca = slice(c * FFN_TILE, (c + 1) * FFN_TILE)
        cg = slice(FFN_DIM + c * FFN_TILE, FFN_DIM + (c + 1) * FFN_TILE)
        a = _dwconv3(_dot(hb_ref[...], wup_ref[:, ca]), cw_ref[:, ca], cb_ref[:, ca], rows)
        g = _dwconv3(_dot(hb_ref[...], wup_ref[:, cg]), cw_ref[:, cg], cb_ref[:, cg], rows)
        z_ref[:, ca] = (_silu(g) * a).astype(BF16)
    y = xm_ref[0] + gate * _dot(z_ref[...], wdn_ref[...])
    if final:
        y = _rms(y, gfin_ref[...])
    out_ref[0] = y


def _ffn_weights(w_up, conv_w, conv_b, w_down):
    return w_up.astype(BF16), conv_w, conv_b.reshape(1, -1), w_down.astype(BF16)


def _conv_ffn(x, mods, cond_row, gain, weights, gain_final, *, final):
    bsz, n, _ = x.shape
    rows = _block_rows(n, FFN_ROWS)
    wup, cw, cb, wdn = weights
    prev, main, nxt = _halo_specs(n, rows)
    return pl.pallas_call(
        functools.partial(_ffn_kernel, rows=rows, final=final),
        grid=(bsz, n // rows),
        in_specs=[prev, main, nxt, _mod_spec(cond_row), _resident((1, D_MODEL)),
                  _resident(wup.shape), _resident(cw.shape), _resident(cb.shape), _resident(wdn.shape),
                  _resident((1, D_MODEL))],
        out_specs=_row_spec(rows, D_MODEL),
        out_shape=jax.ShapeDtypeStruct((bsz, n, D_MODEL), F32),
        scratch_shapes=[pltpu.VMEM((rows + 2 * HALO, D_MODEL), BF16),
                        pltpu.VMEM((rows, FFN_DIM), BF16)],
        compiler_params=_cparams(),
        name="conv_ffn",
    )(x, x, x, mods, gain.reshape(1, D_MODEL), wup, cw, cb, wdn, gain_final.reshape(1, D_MODEL))


def _mlstm_out_kernel(x_ref, hft_ref, hbt_ref, ot_ref, mod_ref, w_ref, out_ref):
    h_sum = hft_ref[0].astype(F32) + hbt_ref[0].astype(F32)
    zt = (h_sum * jax.nn.sigmoid(ot_ref[0].astype(F32))).astype(BF16)
    out_ref[0] = x_ref[0] + _mod_part(mod_ref[0], 2) * _dot_tn(zt, w_ref[...])


def _gla_out_kernel(x_ref, of_ref, ob_ref, r_ref, gn_ref, mod_ref, w_ref, out_ref):
    oo = of_ref[0].astype(F32) + ob_ref[0].astype(F32)
    parts = [_rms(oo[:, h * B_DV:(h + 1) * B_DV], gn_ref[...]) for h in range(B_HEADS)]
    z = (jnp.concatenate(parts, axis=1) * _silu(r_ref[0].astype(F32))).astype(BF16)
    out_ref[0] = x_ref[0] + _mod_part(mod_ref[0], 2) * _dot(z, w_ref[...])


def _na_out_kernel(x_ref, o_ref, mod_ref, w_ref, out_ref):
    out_ref[0] = x_ref[0] + _mod_part(mod_ref[0], 2) * _dot(o_ref[0], w_ref[...])


def _na_out_t_kernel(x_ref, ot_ref, mod_ref, w_ref, out_ref):
    out_ref[0] = x_ref[0] + _mod_part(mod_ref[0], 2) * _dot_tn(ot_ref[0], w_ref[...])


def _mixer_out(kernel_fn, name, x, act_specs, acts, consts, mods, cond_row, w_out):
    bsz, n, _ = x.shape
    rows = _block_rows(n)
    w = w_out.astype(BF16)
    specs = [pl.BlockSpec((1, a.shape[1], rows), lambda b, t: (b, 0, t)) if s == "t"
             else _row_spec(rows, a.shape[-1]) for s, a in zip(act_specs, acts)]
    return pl.pallas_call(
        kernel_fn,
        grid=(bsz, n // rows),
        in_specs=([_row_spec(rows, D_MODEL)] + specs + [_resident(c.shape) for c in consts]
                  + [_mod_spec(cond_row), _resident(w.shape)]),
        out_specs=_row_spec(rows, D_MODEL),
        out_shape=jax.ShapeDtypeStruct((bsz, n, D_MODEL), F32),
        compiler_params=_cparams(),
        name=name,
    )(x, *acts, *consts, mods, w)


def _mlstm_proj_kernel(xp_ref, xm_ref, xn_ref, mod_ref, gain_ref, wqk_ref, cw_ref, cb_ref, wvot_ref, wg_ref,
                       bg_ref, wgt_ref, bgt_ref, q_ref, k_ref, vt_ref, ot_ref, bc_ref, gr_ref, bm_ref, hb_ref,
                       *, rows):
    mod = mod_ref[0]
    shift, scale = _mod_part(mod, 0), _mod_part(mod, 1)
    _fill_halo_block(hb_ref, xp_ref, xm_ref, xn_ref, gain_ref[...], scale, shift, rows)
    hm = hb_ref[HALO:HALO + rows]
    L, H = A_CHUNK, A_HEADS
    gates_c = _dot(hm, wg_ref[...]) + bg_ref[...]
    gates_r = _dot_nt(wgt_ref[...], hm) + bgt_ref[...]
    key = lax.broadcasted_iota(jnp.int32, (L, L), 0)
    qry = lax.broadcasted_iota(jnp.int32, (L, L), 1)
    fwd_mask = jnp.where(qry <= key, 1.0, 0.0).astype(BF16)
    bwd_mask = jnp.where(qry >= key, 1.0, 0.0).astype(BF16)
    fwd_lane = lax.broadcasted_iota(jnp.int32, (L, GATE_PAD), 1) < H
    fwd_row = lax.broadcasted_iota(jnp.int32, (2 * H, L), 0) < H
    lane = lax.broadcasted_iota(jnp.int32, (2 * H, L), 1)
    spans = [slice(c * L, (c + 1) * L) for c in range(rows // L)]
    run_max = []
    for cs in spans:
        logf_c = _log_sigmoid(gates_c[cs, GATE_PAD:])
        g_c = jnp.where(fwd_lane, _cumsum_rows(fwd_mask, logf_c), _cumsum_rows(bwd_mask, logf_c))
        bc_ref[0, cs, :] = gates_c[cs, :GATE_PAD] - g_c
        logf_r = _log_sigmoid(gates_r[2 * H:, cs])
        g_r = jnp.where(fwd_row, _cumsum_cols(logf_r, fwd_mask), _cumsum_cols(logf_r, bwd_mask))
        gr_ref[0, :, cs] = g_r
        run_max.append(gates_r[:2 * H, cs] - g_r)
    dist = 1
    while dist < L:
        for c, rm in enumerate(run_max):
            seen_f = jnp.where(lane >= dist, pltpu.roll(rm, dist, 1), -jnp.inf)
            seen_b = jnp.where(lane < L - dist, pltpu.roll(rm, L - dist, 1), -jnp.inf)
            run_max[c] = jnp.maximum(rm, jnp.where(fwd_row, seen_f, seen_b))
        dist *= 2
    for cs, rm in zip(spans, run_max):
        bm_ref[0, :, cs] = rm

    qk = _silu(_dwconv3(_dot(hb_ref[...], wqk_ref[...]), cw_ref[...], cb_ref[...], rows))
    q_ref[0] = (qk[:, :A_QK] * (A_DK ** -0.5)).astype(BF16)
    k_ref[0] = qk[:, A_QK:].astype(BF16)
    vot = _dot_nt(wvot_ref[...], hm)
    vt_ref[0] = vot[:A_V].astype(BF16)
    ot_ref[0] = vot[A_V:].astype(BF16)


def _mlstm_weights(w_in, b_gate, conv_w, conv_b):
    hh = A_HEADS
    wqk = w_in[:, :2 * A_QK].astype(BF16)
    wvot = w_in[:, 2 * A_QK:2 * A_QK + 2 * A_V].T.astype(BF16)
    wgate = w_in[:, 2 * A_QK + 2 * A_V:]
    pad = GATE_PAD - 2 * hh

    def regroup(t, axis, pad):
        i_f, f_f, i_b, f_b = jnp.split(t, 4, axis=axis)
        widths = [(0, 0)] * (t.ndim - 1) + [(0, pad)]
        return jnp.concatenate([jnp.pad(jnp.concatenate([i_f, i_b], axis), widths),
                                jnp.pad(jnp.concatenate([f_f, f_b], axis), widths)], axis)

    wg = regroup(wgate, 1, pad).astype(BF16)
    bg = regroup(b_gate, 0, pad).reshape(1, 2 * GATE_PAD)
    wgt = regroup(wgate, 1, 0).T.astype(BF16)
    bgt = regroup(b_gate, 0, 0).reshape(4 * hh, 1)
    return wqk, conv_w, conv_b.reshape(1, -1), wvot, wg, bg, wgt, bgt


def _mlstm_proj(x, mods, cond_row, gain, weights):
    bsz, n, _ = x.shape
    rows = _block_rows(n)
    prev, main, nxt = _halo_specs(n, rows)
    t_spec = pl.BlockSpec((1, A_V, rows), lambda b, t: (b, 0, t))
    g_spec = pl.BlockSpec((1, 2 * A_HEADS, rows), lambda b, t: (b, 0, t))
    g_shape = jax.ShapeDtypeStruct((bsz, 2 * A_HEADS, n), F32)
    return pl.pallas_call(
        functools.partial(_mlstm_proj_kernel, rows=rows),
        grid=(bsz, n // rows),
        in_specs=[prev, main, nxt, _mod_spec(cond_row), _resident((1, D_MODEL))]
                 + [_resident(w.shape) for w in weights],
        out_specs=[_row_spec(rows, A_QK), _row_spec(rows, A_QK), t_spec, t_spec,
                   _row_spec(rows, GATE_PAD), g_spec, g_spec],
        out_shape=[jax.ShapeDtypeStruct((bsz, n, A_QK), BF16), jax.ShapeDtypeStruct((bsz, n, A_QK), BF16),
                   jax.ShapeDtypeStruct((bsz, A_V, n), BF16), jax.ShapeDtypeStruct((bsz, A_V, n), BF16),
                   jax.ShapeDtypeStruct((bsz, n, GATE_PAD), F32), g_shape, g_shape],
        scratch_shapes=[pltpu.VMEM((rows + 2 * HALO, D_MODEL), BF16)],
        compiler_params=_cparams(),
        name="mlstm_proj",
    )(x, x, x, mods, gain.reshape(1, D_MODEL), *weights)


def _both_directions(one_direction, n_in, n_out, n_scratch):
    def kernel(*refs, cps):
        cuts = np.cumsum([0, n_in, n_in, n_out, n_out, n_scratch, n_scratch])
        groups = [refs[a:b] for a, b in zip(cuts[:-1], cuts[1:])]
        fwd = groups[0] + groups[2] + groups[4]
        bwd = groups[1] + groups[3] + groups[5]
        for phase in ("init", "main", "final"):
            one_direction(*fwd, reverse=False, cps=cps, phase=phase)
            one_direction(*bwd, reverse=True, cps=cps, phase=phase)
    return kernel


def _mlstm_scan_direction(q_ref, k_ref, vt_ref, bc_ref, gr_ref, bm_ref, s0_ref, m0_ref,
                          ht_ref, s_out_ref, m_out_ref, state_ref, m_ref, *, reverse, cps, phase):
    L, H = A_CHUNK, A_HEADS
    step = pl.program_id(1)

    if phase == "init":
        @pl.when(step == 0)
        def _():
            state_ref[...] = s0_ref[0]
            m_ref[...] = m0_ref[0]
        return
    if phase == "final":
        @pl.when(step == pl.num_programs(1) - 1)
        def _():
            s_out_ref[0] = state_ref[...]
            m_out_ref[0] = m_ref[...]
        return

    key = lax.broadcasted_iota(jnp.int32, (L, L), 0)
    qry = lax.broadcasted_iota(jnp.int32, (L, L), 1)
    keep_t = (key >= qry) if reverse else (key <= qry)
    lane0 = H if reverse else 0
    end = 0 if reverse else L - 1

    order = range(cps - 1, -1, -1) if reverse else range(cps)
    span = lambda c: slice(c * L, (c + 1) * L)
    head_sub = lax.broadcasted_iota(jnp.int32, A_MSTATE, 0)
    head_lane = lax.broadcasted_iota(jnp.int32, A_MSTATE, 1)

    m_prev = m_ref[...]
    mx, inter, floor, decay, b_c, w_c = {}, {}, {}, {}, {}, {}
    for c in order:
        cum_r = gr_ref[0, lane0:lane0 + H, span(c)]
        run_max = bm_ref[0, lane0:lane0 + H, span(c)]
        mx[c] = jnp.maximum(run_max, m_prev)
        mx_end = jnp.broadcast_to(mx[c][:, end:end + 1], A_MSTATE)
        g_tot = jnp.broadcast_to(cum_r[:, end:end + 1], A_MSTATE)
        inter[c] = jnp.exp(m_prev - mx[c])
        floor[c] = jnp.exp(-(cum_r + mx[c]))
        decay[c] = jnp.exp(m_prev - mx_end)
        mx_end_lanes = jnp.sum(jnp.where(head_lane == head_sub + lane0, mx_end, 0.0), axis=0, keepdims=True)
        b_c[c] = bc_ref[0, span(c), :]
        w_c[c] = jnp.exp(b_c[c] - mx_end_lanes)
        m_prev = g_tot + mx_end
    m_ref[...] = m_prev

    items = [(c, h) for c in range(cps) for h in range(H)]
    pairs = [(c, p) for c in range(cps) for p in range(H // 2)]
    ones = jnp.ones((A_ONES, L), BF16)
    low = lax.broadcasted_iota(jnp.int32, (L, 2 * A_DK), 1) < A_DK
    low_row = low[0:1]
    k2 = {(c, p): k_ref[0, span(c), 2 * p * A_DK:2 * (p + 1) * A_DK] for c, p in pairs}
    q2 = {(c, p): q_ref[0, span(c), 2 * p * A_DK:2 * (p + 1) * A_DK] for c, p in pairs}
    q_bd = {i: jnp.concatenate([jnp.where(low, q2[i], 0), jnp.where(low, 0, q2[i])], axis=0).astype(BF16)
            for i in pairs}
    vaug_t = {(c, h): jnp.concatenate([vt_ref[0, h * A_DV:(h + 1) * A_DV, span(c)], ones], axis=0)
              for c, h in items}
    s_pair = {i: _dot_nt(k2[i], q_bd[i]) for i in pairs}
    state_add = {}
    for c, p in pairs:
        col = lambda h: w_c[c][:, lane0 + h:lane0 + h + 1]
        kw = (k2[c, p].astype(F32) * jnp.where(low, col(2 * p), col(2 * p + 1))).astype(BF16)
        state_add[c, p] = (_dot(vaug_t[c, 2 * p], jnp.where(low, kw, 0))
                           + _dot(vaug_t[c, 2 * p + 1], jnp.where(low, 0, kw)))
    p_t = {}
    for c, h in items:
        b_col = b_c[c][:, lane0 + h:lane0 + h + 1]
        w_t = jnp.exp(jnp.where(keep_t, b_col - mx[c][h:h + 1, :], -jnp.inf))
        p_t[c, h] = (s_pair[c, h // 2][:, (h % 2) * L:(h % 2 + 1) * L] * w_t).astype(BF16)
    local_t = {i: _dot(vaug_t[i], p_t[i]) for i in items}

    for p in range(H // 2):
        state = state_ref[p]
        for c in order:
            carry = _dot_nt(state.astype(BF16), q_bd[c, p])
            for h in (2 * p, 2 * p + 1):
                tot = local_t[c, h] + inter[c][h:h + 1, :] * carry[:, (h % 2) * L:(h % 2 + 1) * L]
                den = jnp.maximum(jnp.abs(tot[A_DV:A_DV + 1, :]), floor[c][h:h + 1, :])
                ht_ref[0, h * A_DV:(h + 1) * A_DV, span(c)] = (tot[:A_DV] / den).astype(BF16)
            scale = jnp.where(low_row, decay[c][2 * p:2 * p + 1, :], decay[c][2 * p + 1:2 * p + 2, :])
            state = scale * state + state_add[c, p]
        state_ref[p] = state


def _chunk_order(n_chunks, reverse):
    return (lambda c: n_chunks - 1 - c) if reverse else (lambda c: c)


def _mlstm_scan(q, k, vt, b_c, g_r, bmax_r, carry_f, carry_b):
    bsz, n, _ = q.shape
    cps = min(A_CHUNKS_PER_STEP, n // A_CHUNK)
    L = A_CHUNK * cps
    s_spec = pl.BlockSpec((1,) + A_STATE, lambda b, c: (b, 0, 0, 0))
    m_spec = pl.BlockSpec((1,) + A_MSTATE, lambda b, c: (b, 0, 0))

    def specs(reverse):
        cidx = _chunk_order(n // L, reverse)
        tok = lambda w: pl.BlockSpec((1, L, w), lambda b, c: (b, cidx(c), 0))
        chan = lambda r: pl.BlockSpec((1, r, L), lambda b, c: (b, 0, cidx(c)))
        return ([tok(A_QK), tok(A_QK), chan(A_V), tok(GATE_PAD), chan(2 * A_HEADS), chan(2 * A_HEADS),
                 s_spec, m_spec], [chan(A_V), s_spec, m_spec])

    (in_f, out_f), (in_b, out_b) = specs(False), specs(True)
    shapes = [jax.ShapeDtypeStruct((bsz, A_V, n), BF16), jax.ShapeDtypeStruct((bsz,) + A_STATE, F32),
              jax.ShapeDtypeStruct((bsz,) + A_MSTATE, F32)]
    scratch = [pltpu.VMEM(A_STATE, F32), pltpu.VMEM(A_MSTATE, F32)]
    hf, sf, mf, hb, sb, mb = pl.pallas_call(
        functools.partial(_both_directions(_mlstm_scan_direction, 8, 3, 2), cps=cps),
        grid=(bsz, n // L),
        in_specs=in_f + in_b,
        out_specs=out_f + out_b,
        out_shape=shapes + shapes,
        scratch_shapes=scratch + scratch,
        compiler_params=_cparams(),
        name="mlstm_scan",
    )(q, k, vt, b_c, g_r, bmax_r, *carry_f, q, k, vt, b_c, g_r, bmax_r, *carry_b)
    return hf, hb, (sf, mf), (sb, mb)


def _mlstm_mixer(x_ctx, x_lat, mods, ctx_row, gain, w_in, b_gate, conv_w, conv_b):
    weights = _mlstm_weights(w_in, b_gate, conv_w, conv_b)
    bsz = x_lat.shape[0]
    zero = (jnp.zeros((bsz,) + A_STATE, F32), jnp.zeros((bsz,) + A_MSTATE, F32))
    carry_f, carry_b = zero, zero
    acts = []
    for x, row in ((x_ctx, ctx_row), (x_lat, None)):
        q, k, vt, ot, b_c, g_r, bmax_r = _mlstm_proj(x, mods, row, gain, weights)
        hf, hb, carry_f, carry_b = _mlstm_scan(q, k, vt, b_c, g_r, bmax_r, carry_f, carry_b)
        acts.append((hf, hb, ot))
    return acts[0], acts[1]


def _gla_proj_kernel(x_ref, mod_ref, gain_ref, wqkv_ref, wr_ref, wa_ref, wa2_ref, ba_ref,
                     q_ref, k_ref, v_ref, r_ref, gf_ref, gb_ref, *, rows):
    mod = mod_ref[0]
    h = _norm_mod(x_ref[0], gain_ref[...], _mod_part(mod, 1), _mod_part(mod, 0)).astype(BF16)
    L = B_CHUNK
    a = _dot(h, wa_ref[...]).astype(BF16)
    row = lax.broadcasted_iota(jnp.int32, (L, L), 0)
    col = lax.broadcasted_iota(jnp.int32, (L, L), 1)
    for d, out_ref in enumerate((gf_ref, gb_ref)):
        mask = jnp.where((col >= row) if d else (col <= row), 1.0, 0.0).astype(BF16)
        g = _log_sigmoid(_dot(a, wa2_ref[d]) + ba_ref[d]) / B_TAU
        for c in range(rows // L):
            out_ref[0, c * L:(c + 1) * L, :] = _cumsum_rows(mask, g[c * L:(c + 1) * L])
    qkv = _dot(h, wqkv_ref[...])
    q_ref[0] = qkv[:, :B_QK].astype(BF16)
    k_ref[0] = qkv[:, B_QK:2 * B_QK].astype(BF16)
    v_ref[0] = qkv[:, 2 * B_QK:].astype(BF16)
    r_ref[0] = _dot(h, wr_ref[...]).astype(BF16)


def _gla_proj(x, mods, cond_row, gain, weights):
    bsz, n, _ = x.shape
    rows = _block_rows(n, 1024)
    return pl.pallas_call(
        functools.partial(_gla_proj_kernel, rows=rows),
        grid=(bsz, n // rows),
        in_specs=[_row_spec(rows, D_MODEL), _mod_spec(cond_row), _resident((1, D_MODEL))]
                 + [_resident(w.shape) for w in weights],
        out_specs=[_row_spec(rows, B_QK), _row_spec(rows, B_QK), _row_spec(rows, B_V), _row_spec(rows, B_V),
                   _row_spec(rows, B_QK), _row_spec(rows, B_QK)],
        out_shape=[jax.ShapeDtypeStruct((bsz, n, B_QK), BF16), jax.ShapeDtypeStruct((bsz, n, B_QK), BF16),
                   jax.ShapeDtypeStruct((bsz, n, B_V), BF16), jax.ShapeDtypeStruct((bsz, n, B_V), BF16),
                   jax.ShapeDtypeStruct((bsz, n, B_QK), F32), jax.ShapeDtypeStruct((bsz, n, B_QK), F32)],
        compiler_params=_cparams(),
        name="gla_proj",
    )(x, mods, gain.reshape(1, D_MODEL), *weights)


def _gla_scan_direction(q_ref, k_ref, v_ref, g_ref, s0_ref, o_ref, s_out_ref, state_ref, att_ref,
                        *, reverse, cps, phase):
    L, sub = B_CHUNK, B_SUB
    nsub = L // sub
    step = pl.program_id(1)

    if phase == "init":
        @pl.when(step == 0)
        def _():
            state_ref[...] = s0_ref[0]
        return
    if phase == "final":
        @pl.when(step == pl.num_programs(1) - 1)
        def _():
            s_out_ref[0] = state_ref[...]
        return

    items = [(c, h) for c in range(cps) for h in range(B_HEADS)]
    rows = lambda c: slice(c * L, (c + 1) * L)
    Gs = {(c, h): g_ref[0, rows(c), h * B_DK:(h + 1) * B_DK] for c, h in items}
    qs = {(c, h): q_ref[0, rows(c), h * B_DK:(h + 1) * B_DK].astype(F32) * (B_DK ** -0.5) for c, h in items}
    ks = {(c, h): k_ref[0, rows(c), h * B_DK:(h + 1) * B_DK].astype(F32) for c, h in items}
    vs = {(c, h): v_ref[0, rows(c), h * B_DV:(h + 1) * B_DV] for c, h in items}
    g_tot = {i: (G[0:1] if reverse else G[L - 1:L]) for i, G in Gs.items()}
    q_carry = {i: (qs[i] * jnp.exp(Gs[i])).astype(BF16) for i in items}
    state_add = {i: _dot_tn(vs[i], (ks[i] * jnp.exp(g_tot[i] - Gs[i])).astype(BF16)) for i in items}

    def edge_scores(G, qh, kh, own_block):
        def edge(b):
            first = (b * sub) if own_block != reverse else (b * sub + sub - 1)
            return G[first:first + 1]
        edges = jnp.concatenate([jnp.broadcast_to(edge(b), (sub, B_DK)) for b in range(nsub)], axis=0)
        k_edge = kh * jnp.exp(edges - G)
        q_cols, k_cols = [], []
        for b in range(nsub):
            lo, hi = b * sub, (b + 1) * sub
            if reverse:
                top = lo + sub if own_block else lo
                seen = [qh[:top] * jnp.exp(G[:top] - edge(b)), jnp.zeros((L - top, B_DK), F32)]
            else:
                bot = lo if own_block else hi
                seen = [jnp.zeros((bot, B_DK), F32), qh[bot:] * jnp.exp(G[bot:] - edge(b))]
            q_cols.append(jnp.concatenate([s for s in seen if s.shape[0]], axis=0).astype(BF16))
            k_parts = [jnp.zeros((lo, B_DK), F32), k_edge[lo:hi], jnp.zeros((L - hi, B_DK), F32)]
            k_cols.append(jnp.concatenate([s for s in k_parts if s.shape[0]], axis=0).astype(BF16))
        return _dot_nt(jnp.concatenate(q_cols, axis=1), jnp.concatenate(k_cols, axis=1))

    mild = jnp.min(g_ref[0]) >= -B_MILD_DECAY

    @pl.when(mild)
    def _():
        row = lax.broadcasted_iota(jnp.int32, (L, L), 0)
        col = lax.broadcasted_iota(jnp.int32, (L, L), 1)
        keep = (col >= row) if reverse else (col <= row)
        for n, i in enumerate(items):
            att_ref[n] = jnp.where(keep, edge_scores(Gs[i], qs[i], ks[i], True), 0.0)

    @pl.when(jnp.logical_not(mild))
    def _():
        pair = lax.broadcasted_iota(jnp.int32, (L * sub, 128), 0)
        lane = lax.broadcasted_iota(jnp.int32, (L * sub, 128), 1)
        place = lane == (pair // sub // sub) * sub + pair % sub
        sel = (lax.broadcasted_iota(jnp.int32, (L, L * sub), 1) // sub
               == lax.broadcasted_iota(jnp.int32, (L, L * sub), 0))
        sel_bf16 = jnp.where(sel, 1.0, 0.0).astype(BF16)
        ones_red = jnp.ones((B_DK, 128), BF16)
        srow = lax.broadcasted_iota(jnp.int32, (sub, B_DK), 0)

        def pair_sums(G, qh, kh):
            pieces = []
            for b in range(nsub):
                Gb = G[b * sub:(b + 1) * sub]
                kb = kh[b * sub:(b + 1) * sub]
                for j in range(sub):
                    r = b * sub + j
                    vis = (srow >= j) if reverse else (srow <= j)
                    pieces.append(qh[r:r + 1] * kb * jnp.exp(jnp.where(vis, G[r:r + 1] - Gb, -jnp.inf)))
            return _dot(jnp.concatenate(pieces, axis=0).astype(BF16), ones_red)

        att_cross = {i: edge_scores(Gs[i], qs[i], ks[i], False) for i in items}
        sums = {i: pair_sums(Gs[i], qs[i], ks[i]) for i in items}
        for n, i in enumerate(items):
            local = _dot(sel_bf16, jnp.where(place, sums[i], 0.0).astype(BF16))
            att_ref[n] = att_cross[i] + local[:, :L]

    o_local = {i: _dot(att_ref[n].astype(BF16), vs[i]) for n, i in enumerate(items)}

    order = range(cps - 1, -1, -1) if reverse else range(cps)
    for h in range(B_HEADS):
        state = state_ref[h]
        for c in order:
            o_ref[0, rows(c), h * B_DV:(h + 1) * B_DV] = (
                o_local[c, h] + _dot_nt(q_carry[c, h], state.astype(BF16))).astype(BF16)
            state = state * jnp.exp(g_tot[c, h]) + state_add[c, h]
        state_ref[h] = state


def _gla_scan(q, k, v, g_f, g_b, state_f, state_b):
    bsz, n, _ = q.shape
    cps = min(B_CHUNKS_PER_STEP, n // B_CHUNK)
    L = B_CHUNK * cps
    sshape = (B_HEADS, B_DV, B_DK)
    s_spec = pl.BlockSpec((1,) + sshape, lambda b, c: (b, 0, 0, 0))

    def specs(reverse):
        cidx = _chunk_order(n // L, reverse)
        tok = lambda w: pl.BlockSpec((1, L, w), lambda b, c: (b, cidx(c), 0))
        return [tok(B_QK), tok(B_QK), tok(B_V), tok(B_QK), s_spec], [tok(B_V), s_spec]

    (in_f, out_f), (in_b, out_b) = specs(False), specs(True)
    shapes = [jax.ShapeDtypeStruct((bsz, n, B_V), BF16), jax.ShapeDtypeStruct((bsz,) + sshape, F32)]
    of, sf, ob, sb = pl.pallas_call(
        functools.partial(_both_directions(_gla_scan_direction, 5, 2, 2), cps=cps),
        grid=(bsz, n // L),
        in_specs=in_f + in_b,
        out_specs=out_f + out_b,
        out_shape=shapes + shapes,
        scratch_shapes=[pltpu.VMEM(sshape, F32), pltpu.VMEM((cps * B_HEADS, B_CHUNK, B_CHUNK), F32)] * 2,
        compiler_params=_cparams(),
        name="gla_scan",
    )(q, k, v, g_f, state_f, q, k, v, g_b, state_b)
    return of, ob, sf, sb


def _gla_mixer(x_ctx, x_lat, mods, ctx_row, gain, w_in, w_a2, b_a):
    wqkv = w_in[:, :2 * B_QK + B_V].astype(BF16)
    wr = w_in[:, 2 * B_QK + B_V:2 * B_QK + 2 * B_V].astype(BF16)
    wa = jnp.pad(w_in[:, 2 * B_QK + 2 * B_V:], ((0, 0), (0, GATE_PAD - 2 * B_RANK))).astype(BF16)
    wa2 = jnp.stack([jnp.zeros((GATE_PAD, B_QK), F32).at[d * B_RANK:(d + 1) * B_RANK].set(w_a2[d])
                     for d in range(2)]).astype(BF16)
    ba = b_a.reshape(2, 1, B_QK)
    bsz = x_lat.shape[0]
    state_f = state_b = jnp.zeros((bsz, B_HEADS, B_DV, B_DK), F32)
    acts = []
    for x, row in ((x_ctx, ctx_row), (x_lat, None)):
        q, k, v, r, g_f, g_b = _gla_proj(x, mods, row, gain, (wqkv, wr, wa, wa2, ba))
        of, ob, state_f, state_b = _gla_scan(q, k, v, g_f, g_b, state_f, state_b)
        acts.append((of, ob, r))
    return acts[0], acts[1]


def _na_proj_kernel_ctx(x_ref, mod_ref, gain_ref, w_ref, wvt_ref, q_ref, k_ref, v_ref, vt_ref):
    mod = mod_ref[0]
    h = _norm_mod(x_ref[0], gain_ref[...], _mod_part(mod, 1), _mod_part(mod, 0)).astype(BF16)
    qkv = _dot(h, w_ref[...])
    q_ref[0] = (qkv[:, :D_MODEL] * C_QSCALE).astype(BF16)
    k_ref[0] = qkv[:, D_MODEL:2 * D_MODEL].astype(BF16)
    v_ref[0] = qkv[:, 2 * D_MODEL:].astype(BF16)
    vt_ref[0] = _dot_nt(wvt_ref[...], h).astype(BF16)


def _na_proj_kernel_lat(x_ref, mod_ref, gain_ref, w_ref, wvt_ref, q_ref, k_ref, vt_ref):
    mod = mod_ref[0]
    h = _norm_mod(x_ref[0], gain_ref[...], _mod_part(mod, 1), _mod_part(mod, 0)).astype(BF16)
    qk = _dot(h, w_ref[...])
    q_ref[0] = (qk[:, :D_MODEL] * C_QSCALE).astype(BF16)
    k_ref[0] = qk[:, D_MODEL:].astype(BF16)
    vt_ref[0] = _dot_nt(wvt_ref[...], h).astype(BF16)


def _na_proj(x, mods, cond_row, gain, w_in, *, with_v):
    bsz, n, _ = x.shape
    rows = _block_rows(n, 1024)
    wvt = w_in[:, 2 * D_MODEL:].T.astype(BF16)
    w = (w_in if with_v else w_in[:, :2 * D_MODEL]).astype(BF16)
    tok = jax.ShapeDtypeStruct((bsz, n, D_MODEL), BF16)
    vt_shape = jax.ShapeDtypeStruct((bsz, D_MODEL, n), BF16)
    vt_spec = pl.BlockSpec((1, D_MODEL, rows), lambda b, t: (b, 0, t))
    rs = _row_spec(rows, D_MODEL)
    return pl.pallas_call(
        _na_proj_kernel_ctx if with_v else _na_proj_kernel_lat,
        grid=(bsz, n // rows),
        in_specs=[rs, _mod_spec(cond_row), _resident((1, D_MODEL)), _resident(w.shape), _resident(wvt.shape)],
        out_specs=[rs, rs, rs, vt_spec] if with_v else [rs, rs, vt_spec],
        out_shape=[tok, tok, tok, vt_shape] if with_v else [tok, tok, vt_shape],
        compiler_params=_cparams(),
        name="na_proj_ctx" if with_v else "na_proj_lat",
    )(x, mods, gain.reshape(1, D_MODEL), w, wvt)


def _na_bias_table(rpb):
    colq = np.arange(GRID_W)[None, :]
    colk = np.arange(GRID_W)[:, None]
    start = np.clip(colq - C_WIN_COLS // 2, 0, GRID_W - C_WIN_COLS)
    valid = (colk >= start) & (colk < start + C_WIN_COLS)
    w2 = 2 * GRID_W
    lead = GRID_W - C_WIN_COLS
    padded = jnp.pad(rpb, ((0, 0), (0, 0), (lead, w2 - lead - rpb.shape[-1])))
    skew = jnp.tile(padded, (1, 1, GRID_W))[..., :GRID_W * (w2 - 1)]
    skew = skew.reshape(rpb.shape[0], rpb.shape[1], GRID_W, w2 - 1)[..., GRID_W - 1:]
    bias = jnp.where(valid[None, None], jnp.swapaxes(skew, 2, 3) * LOG2E, -jnp.inf)
    bias = bias.transpose(1, 2, 0, 3).reshape(2 * C_WIN_ROWS - 1, GRID_W, C_HEADS * GRID_W)
    return jnp.concatenate([jnp.full((1,) + bias.shape[1:], -jnp.inf, F32), bias], axis=0)


def _na_lat_kernel(q_ref, k_ref, kc_ref, vt_ref, vtc_ref, tab_ref, ot_ref, *, rows):
    r0 = 2 * pl.program_id(1)
    gw = C_GROUP * C_DH
    nk = C_KROWS * GRID_W
    first = lambda r: jnp.clip(r - C_WIN_ROWS // 2, 0, rows - C_WIN_ROWS)
    rs_load = jnp.minimum(first(r0) - first(r0) % 2, rows - C_KROWS)
    start = pl.multiple_of(rs_load * GRID_W, 2 * GRID_W)
    entries = []
    for i in range(C_KROWS):
        kr = rs_load + i
        per_q = []
        for r in (r0, r0 + 1):
            ok = jnp.logical_and(kr >= first(r), kr < first(r) + C_WIN_ROWS)
            per_q.append(jnp.where(ok, kr - r + C_WIN_ROWS, 0))
        entries.append(per_q)

    blk_r = lax.broadcasted_iota(jnp.int32, (gw, gw), 0) // C_DH
    blk_c = lax.broadcasted_iota(jnp.int32, (gw, gw), 1) // C_DH
    same_head = blk_r == blk_c

    def block_diag(qg):
        return jnp.where(same_head, jnp.concatenate([qg] * C_GROUP, axis=0), 0).astype(BF16)

    n_groups = C_HEADS // C_GROUP
    gsl = [slice(g * gw, (g + 1) * gw) for g in range(n_groups)]
    for w0 in range(0, n_groups, C_WAVE):
        wave = range(w0, min(w0 + C_WAVE, n_groups))
        q_bd = {g: jnp.concatenate([block_diag(q_ref[0, :GRID_W, gsl[g]]),
                                    block_diag(q_ref[0, GRID_W:, gsl[g]])], axis=0)
                for g in wave}
        s_loc = {g: _dot_nt(k_ref[0, pl.ds(start, nk), gsl[g]], q_bd[g]) for g in wave}
        s_ctx = {g: _dot_nt(kc_ref[0, :, gsl[g]], q_bd[g]) for g in wave}
        p_loc, p_ctx, denom = {}, {}, {}
        for g in wave:
            bias = jnp.concatenate(
                [jnp.concatenate([tab_ref[e, :, gsl[g]] for e in per_q], axis=1) for per_q in entries], axis=0)
            sl = s_loc[g] + bias
            m = jnp.maximum(jnp.max(sl, axis=0, keepdims=True), jnp.max(s_ctx[g], axis=0, keepdims=True))
            pl_, pc_ = jnp.exp2(sl - m), jnp.exp2(s_ctx[g] - m)
            denom[g] = jnp.sum(pl_, axis=0, keepdims=True) + jnp.sum(pc_, axis=0, keepdims=True)
            p_loc[g] = pl_.astype(BF16)
            p_ctx[g] = pc_.astype(BF16)
        for g in wave:
            ot = (_dot(vt_ref[0, gsl[g], pl.ds(start, nk)], p_loc[g])
                  + _dot(vtc_ref[0, gsl[g], :], p_ctx[g])) / denom[g]
            for qr in range(2):
                for h in range(C_GROUP):
                    ch = slice(g * gw + h * C_DH, g * gw + (h + 1) * C_DH)
                    ot_ref[0, ch, qr * GRID_W:(qr + 1) * GRID_W] = (
                        ot[h * C_DH:(h + 1) * C_DH, qr * gw + h * C_DH:qr * gw + (h + 1) * C_DH].astype(BF16))


def _na_lat_attention(q, k, kc, vt, vtc, tab):
    bsz, n, _ = q.shape
    rows = n // GRID_W
    n_ctx = kc.shape[1]
    assert rows >= C_KROWS and rows % 2 == 0
    return pl.pallas_call(
        functools.partial(_na_lat_kernel, rows=rows),
        grid=(bsz, rows // 2),
        in_specs=[pl.BlockSpec((1, 2 * GRID_W, D_MODEL), lambda b, r: (b, r, 0)),
                  pl.BlockSpec((1, n, D_MODEL), lambda b, r: (b, 0, 0)),
                  pl.BlockSpec((1, n_ctx, D_MODEL), lambda b, r: (b, 0, 0)),
                  pl.BlockSpec((1, D_MODEL, n), lambda b, r: (b, 0, 0)),
                  pl.BlockSpec((1, D_MODEL, n_ctx), lambda b, r: (b, 0, 0)),
                  _resident(tab.shape)],
        out_specs=pl.BlockSpec((1, D_MODEL, 2 * GRID_W), lambda b, r: (b, 0, r)),
        out_shape=jax.ShapeDtypeStruct((bsz, D_MODEL, n), BF16),
        compiler_params=_cparams(),
        name="na_lat_attention",
    )(q, k, kc, vt, vtc, tab)


def _na_ctx_kernel(q_ref, k_ref, v_ref, o_ref):
    hsl = [slice(h * C_DH, (h + 1) * C_DH) for h in range(C_HEADS)]
    s = [_dot_nt(q_ref[0, :, hs], k_ref[0, :, hs]) for hs in hsl]
    p = [jnp.exp2(sh - jnp.max(sh, axis=-1, keepdims=True)) for sh in s]
    o = [_dot(ph.astype(BF16), v_ref[0, :, hs]) for ph, hs in zip(p, hsl)]
    for oh, ph, hs in zip(o, p, hsl):
        o_ref[0, :, hs] = (oh / jnp.sum(ph, axis=-1, keepdims=True)).astype(BF16)


def _na_ctx_attention(q, k, v):
    bsz, n, _ = q.shape
    spec = pl.BlockSpec((1, n, D_MODEL), lambda b: (b, 0, 0))
    return pl.pallas_call(
        _na_ctx_kernel,
        grid=(bsz,),
        in_specs=[spec, spec, spec],
        out_specs=spec,
        out_shape=jax.ShapeDtypeStruct((bsz, n, D_MODEL), BF16),
        compiler_params=_cparams(1),
        name="na_ctx_attention",
    )(q, k, v)


def kernel(x, c, ctx, c_ctx, w_ada, b_ada, norm_mix, norm_ffn, w_up, ffn_conv_w, ffn_conv_b, w_down,
           a_w_in, a_b_gate, a_conv_w, a_conv_b, a_w_out,
           b_w_in, b_w_a2, b_b_a, b_norm, b_w_out,
           c_w_in, c_rpb, c_w_out, norm_final):
    bsz = x.shape[0]
    depth = w_ada.shape[0]
    ctx_row = bsz
    assert bsz < COND_ROWS

    cond = jnp.concatenate([c, c_ctx[None], jnp.zeros((COND_ROWS - bsz - 1, D_MODEL), F32)], axis=0)
    mod_all = _ada_modulation(cond, w_ada, b_ada)
    x_ctx, x_lat = ctx, x

    for i in range(depth):
        last = i == depth - 1
        mods = mod_all[i].reshape(COND_ROWS, 1, 6 * D_MODEL)
        kind, j = i % 3, i // 3
        if kind == 0:
            acts_c, acts_l = _mlstm_mixer(x_ctx, x_lat, mods, ctx_row, norm_mix[i], a_w_in[j], a_b_gate[j],
                                          a_conv_w[j], a_conv_b[j])
            out = functools.partial(_mixer_out, _mlstm_out_kernel, "mlstm_out", consts=[], mods=mods,
                                    w_out=a_w_out[j], act_specs="ttt")
        elif kind == 1:
            acts_c, acts_l = _gla_mixer(x_ctx, x_lat, mods, ctx_row, norm_mix[i], b_w_in[j], b_w_a2[j], b_b_a[j])
            out = functools.partial(_mixer_out, _gla_out_kernel, "gla_out", consts=[b_norm[j].reshape(1, B_DV)],
                                    mods=mods, w_out=b_w_out[j], act_specs="rrr")
        else:
            qc, kc, vc, vtc = _na_proj(x_ctx, mods, ctx_row, norm_mix[i], c_w_in[j], with_v=True)
            ql, kl, vtl = _na_proj(x_lat, mods, None, norm_mix[i], c_w_in[j], with_v=False)
            ot = _na_lat_attention(ql, kl, kc, vtl, vtc, _na_bias_table(c_rpb[j]))
            mid_lat = _mixer_out(_na_out_t_kernel, "na_out_lat", x_lat, "t", [ot], [], mods, None, c_w_out[j])
            if not last:
                oc = _na_ctx_attention(qc, kc, vc)
                mid_ctx = _mixer_out(_na_out_kernel, "na_out_ctx", x_ctx, "r", [oc], [], mods, ctx_row, c_w_out[j])
        if kind != 2:
            mid_lat = out(x=x_lat, acts=list(acts_l), cond_row=None)
            if not last:
                mid_ctx = out(x=x_ctx, acts=list(acts_c), cond_row=ctx_row)
        ffn_w = _ffn_weights(w_up[i], ffn_conv_w[i], ffn_conv_b[i], w_down[i])
        x_lat = _conv_ffn(mid_lat, mods, None, norm_ffn[i], ffn_w, norm_final, final=last)
        if not last:
            x_ctx = _conv_ffn(mid_ctx, mods, ctx_row, norm_ffn[i], ffn_w, norm_final, final=False)
    return x_lat
```

```python
import functools

import jax
import jax.numpy as jnp
import numpy as np
from jax import lax
from jax.experimental import pallas as pl
from jax.experimental.pallas import tpu as pltpu

F32 = jnp.float32
BF16 = jnp.bfloat16

D_MODEL = 1024
EPS = 1e-6
GRID_W = 64

HALO = 16
VMEM_LIMIT = 56 * 1024 * 1024
COND_ROWS = 16

A_HEADS, A_DK, A_DV, A_CHUNK = 8, 64, 128, 128
A_QK, A_V = A_HEADS * A_DK, A_HEADS * A_DV
A_ONES = 16
A_CHUNKS_PER_STEP = 4
A_STATE = (A_HEADS // 2, A_DV + A_ONES, 2 * A_DK)
A_MSTATE = (A_HEADS, 128)
B_HEADS, B_DK, B_DV, B_CHUNK = 4, 128, 256, 64
B_QK, B_V, B_RANK, B_TAU = B_HEADS * B_DK, B_HEADS * B_DV, 16, 16.0
B_SUB = 8
B_CHUNKS_PER_STEP = 4
B_MILD_DECAY = 60.0
C_HEADS, C_DH, C_WIN_ROWS, C_WIN_COLS = 16, 64, 8, 16
C_GROUP = 2
C_WAVE = 4
LOG2E = 1.4426950408889634
C_QSCALE = C_DH ** -0.5 * LOG2E
C_KROWS = C_WIN_ROWS + 2
FFN_DIM = 2816
FFN_TILE = 256
FFN_ROWS = 1024
GATE_PAD = 128

NT_DIMS = (((1,), (1,)), ((), ()))
TN_DIMS = (((0,), (0,)), ((), ()))


def _cparams(n_axes=2):
    return pltpu.CompilerParams(dimension_semantics=("arbitrary",) * n_axes, vmem_limit_bytes=VMEM_LIMIT)


def _resident(shape):
    nd = len(shape)
    return pl.BlockSpec(shape, lambda *_: (0,) * nd, pipeline_mode=pl.Buffered(1))


def _block_rows(n_tokens, largest=512):
    return max(r for r in (256, 512, 1024) if r <= largest and (n_tokens % r == 0 or r == 256))


def _dot(a, b):
    return jnp.dot(a, b, preferred_element_type=F32)


def _dot_nt(a, b):
    return lax.dot_general(a, b, NT_DIMS, preferred_element_type=F32)


def _dot_tn(a, b):
    return lax.dot_general(a, b, TN_DIMS, preferred_element_type=F32)


def _split3(x):
    x1 = x.astype(BF16)
    r1 = x - x1.astype(F32)
    x2 = r1.astype(BF16)
    x3 = (r1 - x2.astype(F32)).astype(BF16)
    return x1, x2, x3


def _cumsum_rows(mask_bf16, x):
    x1, x2, x3 = _split3(x)
    return _dot(mask_bf16, x1) + _dot(mask_bf16, x2) + _dot(mask_bf16, x3)


def _cumsum_cols(x, mask_bf16):
    x1, x2, x3 = _split3(x)
    return _dot_nt(x1, mask_bf16) + _dot_nt(x2, mask_bf16) + _dot_nt(x3, mask_bf16)


def _log_sigmoid(z):
    return jnp.minimum(z, 0.0) - jnp.log1p(jnp.exp(-jnp.abs(z)))


def _silu(z):
    return z * jax.nn.sigmoid(z)


def _rms(x, gain):
    return x * lax.rsqrt(jnp.mean(x * x, axis=-1, keepdims=True) + EPS) * gain


def _norm_mod(x, gain, scale, shift):
    return _rms(x, gain) * (1.0 + scale) + shift


def _mod_part(mod, i):
    return mod[:, i * D_MODEL:(i + 1) * D_MODEL]


def _dwconv3(u, w, b, rows):
    n = rows + 2 * HALO
    prev = pltpu.roll(u, 1, 0)[HALO:HALO + rows]
    nxt = pltpu.roll(u, n - 1, 0)[HALO:HALO + rows]
    return prev * w[0:1] + u[HALO:HALO + rows] * w[1:2] + nxt * w[2:3] + b


def _fill_halo_block(hb_ref, xp_ref, xm_ref, xn_ref, gain, scale, shift, rows):
    t = pl.program_id(1)
    has_prev = (t != 0).astype(F32)
    has_next = (t != pl.num_programs(1) - 1).astype(F32)
    hb_ref[0:HALO] = (_norm_mod(xp_ref[0], gain, scale, shift) * has_prev).astype(BF16)
    hb_ref[HALO:HALO + rows] = _norm_mod(xm_ref[0], gain, scale, shift).astype(BF16)
    hb_ref[HALO + rows:] = (_norm_mod(xn_ref[0], gain, scale, shift) * has_next).astype(BF16)


def _halo_specs(n_tokens, rows):
    per = rows // HALO
    last = n_tokens // HALO - 1
    prev = pl.BlockSpec((1, HALO, D_MODEL), lambda b, t: (b, jnp.maximum(t * per - 1, 0), 0))
    main = pl.BlockSpec((1, rows, D_MODEL), lambda b, t: (b, t, 0))
    nxt = pl.BlockSpec((1, HALO, D_MODEL), lambda b, t: (b, jnp.minimum((t + 1) * per, last), 0))
    return prev, main, nxt


def _mod_spec(cond_row):
    if cond_row is None:
        return pl.BlockSpec((1, 1, 6 * D_MODEL), lambda b, t: (b, 0, 0))
    return pl.BlockSpec((1, 1, 6 * D_MODEL), lambda b, t: (cond_row, 0, 0))


def _row_spec(rows, width):
    return pl.BlockSpec((1, rows, width), lambda b, t: (b, t, 0))


def _ada_kernel(c_ref, w_ref, b_ref, o_ref):
    s = _silu(c_ref[...]).astype(BF16)
    o_ref[0] = _dot(s, w_ref[0].astype(BF16)) + b_ref[0]


def _ada_modulation(cond, w_ada, b_ada):
    depth, _, n = w_ada.shape
    tn = 1024
    return pl.pallas_call(
        _ada_kernel,
        grid=(depth, n // tn),
        in_specs=[pl.BlockSpec((COND_ROWS, D_MODEL), lambda l, j: (0, 0)),
                  pl.BlockSpec((1, D_MODEL, tn), lambda l, j: (l, 0, j)),
                  pl.BlockSpec((1, 1, tn), lambda l, j: (l, 0, j))],
        out_specs=pl.BlockSpec((1, COND_ROWS, tn), lambda l, j: (l, 0, j)),
        out_shape=jax.ShapeDtypeStruct((depth, COND_ROWS, n), F32),
        compiler_params=_cparams(),
        name="ada_modulation",
    )(cond, w_ada, b_ada.reshape(depth, 1, n))


def _ffn_kernel(xp_ref, xm_ref, xn_ref, mod_ref, gain_ref, wup_ref, cw_ref, cb_ref, wdn_ref, gfin_ref,
                out_ref, hb_ref, z_ref, *, rows, final):
    mod = mod_ref[0]
    shift, scale, gate = _mod_part(mod, 3), _mod_part(mod, 4), _mod_part(mod, 5)
    _fill_halo_block(hb_ref, xp_ref, xm_ref, xn_ref, gain_ref[...], scale, shift, rows)
    for c in range(FFN_DIM // FFN_TILE):
        ca = slice(c * FFN_TILE, (c + 1) * FFN_TILE)
        cg = slice(FFN_DIM + c * FFN_TILE, FFN_DIM + (c + 1) * FFN_TILE)
        a = _dwconv3(_dot(hb_ref[...], wup_ref[:, ca]), cw_ref[:, ca], cb_ref[:, ca], rows)
        g = _dwconv3(_dot(hb_ref[...], wup_ref[:, cg]), cw_ref[:, cg], cb_ref[:, cg], rows)
        z_ref[:, ca] = (_silu(g) * a).astype(BF16)
    y = xm_ref[0] + gate * _dot(z_ref[...], wdn_ref[...])
    if final:
        y = _rms(y, gfin_ref[...])
    out_ref[0] = y


def _ffn_weights(w_up, conv_w, conv_b, w_down):
    return w_up.astype(BF16), conv_w, conv_b.reshape(1, -1), w_down.astype(BF16)


def _conv_ffn(x, mods, cond_row, gain, weights, gain_final, *, final):
    bsz, n, _ = x.shape
    rows = _block_rows(n, FFN_ROWS)
    wup, cw, cb, wdn = weights
    prev, main, nxt = _halo_specs(n, rows)
    return pl.pallas_call(
        functools.partial(_ffn_kernel, rows=rows, final=final),
        grid=(bsz, n // rows),
        in_specs=[prev, main, nxt, _mod_spec(cond_row), _resident((1, D_MODEL)),
                  _resident(wup.shape), _resident(cw.shape), _resident(cb.shape), _resident(wdn.shape),
                  _resident((1, D_MODEL))],
        out_specs=_row_spec(rows, D_MODEL),
        out_shape=jax.ShapeDtypeStruct((bsz, n, D_MODEL), F32),
        scratch_shapes=[pltpu.VMEM((rows + 2 * HALO, D_MODEL), BF16),
                        pltpu.VMEM((rows, FFN_DIM), BF16)],
        compiler_params=_cparams(),
        name="conv_ffn",
    )(x, x, x, mods, gain.reshape(1, D_MODEL), wup, cw, cb, wdn, gain_final.reshape(1, D_MODEL))


def _mlstm_out_kernel(x_ref, hft_ref, hbt_ref, ot_ref, mod_ref, w_ref, out_ref):
    h_sum = hft_ref[0].astype(F32) + hbt_ref[0].astype(F32)
    zt = (h_sum * jax.nn.sigmoid(ot_ref[0].astype(F32))).astype(BF16)
    out_ref[0] = x_ref[0] + _mod_part(mod_ref[0], 2) * _dot_tn(zt, w_ref[...])


def _gla_out_kernel(x_ref, of_ref, ob_ref, r_ref, gn_ref, mod_ref, w_ref, out_ref):
    oo = of_ref[0].astype(F32) + ob_ref[0].astype(F32)
    parts = [_rms(oo[:, h * B_DV:(h + 1) * B_DV], gn_ref[...]) for h in range(B_HEADS)]
    z = (jnp.concatenate(parts, axis=1) * _silu(r_ref[0].astype(F32))).astype(BF16)
    out_ref[0] = x_ref[0] + _mod_part(mod_ref[0], 2) * _dot(z, w_ref[...])


def _na_out_kernel(x_ref, o_ref, mod_ref, w_ref, out_ref):
    out_ref[0] = x_ref[0] + _mod_part(mod_ref[0], 2) * _dot(o_ref[0], w_ref[...])


def _na_out_t_kernel(x_ref, ot_ref, mod_ref, w_ref, out_ref):
    out_ref[0] = x_ref[0] + _mod_part(mod_ref[0], 2) * _dot_tn(ot_ref[0], w_ref[...])


def _mixer_out(kernel_fn, name, x, act_specs, acts, consts, mods, cond_row, w_out):
    bsz, n, _ = x.shape
    rows = _block_rows(n, 1024)
    w = w_out.astype(BF16)
    specs = [pl.BlockSpec((1, a.shape[1], rows), lambda b, t: (b, 0, t)) if s == "t"
             else _row_spec(rows, a.shape[-1]) for s, a in zip(act_specs, acts)]
    return pl.pallas_call(
        kernel_fn,
        grid=(bsz, n // rows),
        in_specs=([_row_spec(rows, D_MODEL)] + specs + [_resident(c.shape) for c in consts]
                  + [_mod_spec(cond_row), _resident(w.shape)]),
        out_specs=_row_spec(rows, D_MODEL),
        out_shape=jax.ShapeDtypeStruct((bsz, n, D_MODEL), F32),
        compiler_params=_cparams(),
        name=name,
    )(x, *acts, *consts, mods, w)


def _mlstm_proj_kernel(xp_ref, xm_ref, xn_ref, mod_ref, gain_ref, wqk_ref, cw_ref, cb_ref, wvot_ref, wg_ref,
                       bg_ref, wgt_ref, bgt_ref, q_ref, k_ref, vt_ref, ot_ref, bc_ref, gr_ref, bm_ref, hb_ref,
                       *, rows):
    mod = mod_ref[0]
    shift, scale = _mod_part(mod, 0), _mod_part(mod, 1)
    _fill_halo_block(hb_ref, xp_ref, xm_ref, xn_ref, gain_ref[...], scale, shift, rows)
    hm = hb_ref[HALO:HALO + rows]
    L, H = A_CHUNK, A_HEADS
    gates_c = _dot(hm, wg_ref[...]) + bg_ref[...]
    gates_r = _dot_nt(wgt_ref[...], hm) + bgt_ref[...]
    key = lax.broadcasted_iota(jnp.int32, (L, L), 0)
    qry = lax.broadcasted_iota(jnp.int32, (L, L), 1)
    fwd_mask = jnp.where(qry <= key, 1.0, 0.0).astype(BF16)
    bwd_mask = jnp.where(qry >= key, 1.0, 0.0).astype(BF16)
    fwd_lane = lax.broadcasted_iota(jnp.int32, (L, GATE_PAD), 1) < H
    fwd_row = lax.broadcasted_iota(jnp.int32, (2 * H, L), 0) < H
    lane = lax.broadcasted_iota(jnp.int32, (2 * H, L), 1)
    spans = [slice(c * L, (c + 1) * L) for c in range(rows // L)]
    run_max = []
    for cs in spans:
        logf_c = _log_sigmoid(gates_c[cs, GATE_PAD:])
        g_c = jnp.where(fwd_lane, _cumsum_rows(fwd_mask, logf_c), _cumsum_rows(bwd_mask, logf_c))
        bc_ref[0, cs, :] = gates_c[cs, :GATE_PAD] - g_c
        logf_r = _log_sigmoid(gates_r[2 * H:, cs])
        g_r = jnp.where(fwd_row, _cumsum_cols(logf_r, fwd_mask), _cumsum_cols(logf_r, bwd_mask))
        gr_ref[0, :, cs] = g_r
        run_max.append(gates_r[:2 * H, cs] - g_r)
    dist = 1
    while dist < L:
        for c, rm in enumerate(run_max):
            seen_f = jnp.where(lane >= dist, pltpu.roll(rm, dist, 1), -jnp.inf)
            seen_b = jnp.where(lane < L - dist, pltpu.roll(rm, L - dist, 1), -jnp.inf)
            run_max[c] = jnp.maximum(rm, jnp.where(fwd_row, seen_f, seen_b))
        dist *= 2
    for cs, rm in zip(spans, run_max):
        bm_ref[0, :, cs] = rm

    qk = _silu(_dwconv3(_dot(hb_ref[...], wqk_ref[...]), cw_ref[...], cb_ref[...], rows))
    q_ref[0] = (qk[:, :A_QK] * (A_DK ** -0.5)).astype(BF16)
    k_ref[0] = qk[:, A_QK:].astype(BF16)
    vot = _dot_nt(wvot_ref[...], hm)
    vt_ref[0] = vot[:A_V].astype(BF16)
    ot_ref[0] = vot[A_V:].astype(BF16)


def _mlstm_weights(w_in, b_gate, conv_w, conv_b):
    hh = A_HEADS
    wqk = w_in[:, :2 * A_QK].astype(BF16)
    wvot = w_in[:, 2 * A_QK:2 * A_QK + 2 * A_V].T.astype(BF16)
    wgate = w_in[:, 2 * A_QK + 2 * A_V:]
    pad = GATE_PAD - 2 * hh

    def regroup(t, axis, pad):
        i_f, f_f, i_b, f_b = jnp.split(t, 4, axis=axis)
        widths = [(0, 0)] * (t.ndim - 1) + [(0, pad)]
        return jnp.concatenate([jnp.pad(jnp.concatenate([i_f, i_b], axis), widths),
                                jnp.pad(jnp.concatenate([f_f, f_b], axis), widths)], axis)

    wg = regroup(wgate, 1, pad).astype(BF16)
    bg = regroup(b_gate, 0, pad).reshape(1, 2 * GATE_PAD)
    wgt = regroup(wgate, 1, 0).T.astype(BF16)
    bgt = regroup(b_gate, 0, 0).reshape(4 * hh, 1)
    return wqk, conv_w, conv_b.reshape(1, -1), wvot, wg, bg, wgt, bgt


def _mlstm_proj(x, mods, cond_row, gain, weights):
    bsz, n, _ = x.shape
    rows = _block_rows(n, 1024)
    prev, main, nxt = _halo_specs(n, rows)
    t_spec = pl.BlockSpec((1, A_V, rows), lambda b, t: (b, 0, t))
    g_spec = pl.BlockSpec((1, 2 * A_HEADS, rows), lambda b, t: (b, 0, t))
    g_shape = jax.ShapeDtypeStruct((bsz, 2 * A_HEADS, n), F32)
    return pl.pallas_call(
        functools.partial(_mlstm_proj_kernel, rows=rows),
        grid=(bsz, n // rows),
        in_specs=[prev, main, nxt, _mod_spec(cond_row), _resident((1, D_MODEL))]
                 + [_resident(w.shape) for w in weights],
        out_specs=[_row_spec(rows, A_QK), _row_spec(rows, A_QK), t_spec, t_spec,
                   _row_spec(rows, GATE_PAD), g_spec, g_spec],
        out_shape=[jax.ShapeDtypeStruct((bsz, n, A_QK), BF16), jax.ShapeDtypeStruct((bsz, n, A_QK), BF16),
                   jax.ShapeDtypeStruct((bsz, A_V, n), BF16), jax.ShapeDtypeStruct((bsz, A_V, n), BF16),
                   jax.ShapeDtypeStruct((bsz, n, GATE_PAD), F32), g_shape, g_shape],
        scratch_shapes=[pltpu.VMEM((rows + 2 * HALO, D_MODEL), BF16)],
        compiler_params=_cparams(),
        name="mlstm_proj",
    )(x, x, x, mods, gain.reshape(1, D_MODEL), *weights)


def _both_directions(one_direction, n_in, n_out, n_scratch):
    def kernel(*refs, cps):
        cuts = np.cumsum([0, n_in, n_in, n_out, n_out, n_scratch, n_scratch])
        groups = [refs[a:b] for a, b in zip(cuts[:-1], cuts[1:])]
        fwd = groups[0] + groups[2] + groups[4]
        bwd = groups[1] + groups[3] + groups[5]
        for phase in ("init", "main", "final"):
            one_direction(*fwd, reverse=False, cps=cps, phase=phase)
            one_direction(*bwd, reverse=True, cps=cps, phase=phase)
    return kernel


def _mlstm_scan_direction(q_ref, k_ref, vt_ref, bc_ref, gr_ref, bm_ref, s0_ref, m0_ref,
                          ht_ref, s_out_ref, m_out_ref, state_ref, m_ref, *, reverse, cps, phase):
    L, H = A_CHUNK, A_HEADS
    step = pl.program_id(1)

    if phase == "init":
        @pl.when(step == 0)
        def _():
            state_ref[...] = s0_ref[0]
            m_ref[...] = m0_ref[0]
        return
    if phase == "final":
        @pl.when(step == pl.num_programs(1) - 1)
        def _():
            s_out_ref[0] = state_ref[...]
            m_out_ref[0] = m_ref[...]
        return

    key = lax.broadcasted_iota(jnp.int32, (L, L), 0)
    qry = lax.broadcasted_iota(jnp.int32, (L, L), 1)
    keep_t = (key >= qry) if reverse else (key <= qry)
    lane0 = H if reverse else 0
    end = 0 if reverse else L - 1

    order = range(cps - 1, -1, -1) if reverse else range(cps)
    span = lambda c: slice(c * L, (c + 1) * L)
    head_sub = lax.broadcasted_iota(jnp.int32, A_MSTATE, 0)
    head_lane = lax.broadcasted_iota(jnp.int32, A_MSTATE, 1)

    m_prev = m_ref[...]
    mx, inter, floor, decay, b_c, w_c = {}, {}, {}, {}, {}, {}
    for c in order:
        cum_r = gr_ref[0, lane0:lane0 + H, span(c)]
        run_max = bm_ref[0, lane0:lane0 + H, span(c)]
        mx[c] = jnp.maximum(run_max, m_prev)
        mx_end = jnp.broadcast_to(mx[c][:, end:end + 1], A_MSTATE)
        g_tot = jnp.broadcast_to(cum_r[:, end:end + 1], A_MSTATE)
        inter[c] = jnp.exp(m_prev - mx[c])
        floor[c] = jnp.exp(-(cum_r + mx[c]))
        decay[c] = jnp.exp(m_prev - mx_end)
        mx_end_lanes = jnp.sum(jnp.where(head_lane == head_sub + lane0, mx_end, 0.0), axis=0, keepdims=True)
        b_c[c] = bc_ref[0, span(c), :]
        w_c[c] = jnp.exp(b_c[c] - mx_end_lanes)
        m_prev = g_tot + mx_end
    m_ref[...] = m_prev

    items = [(c, h) for c in range(cps) for h in range(H)]
    pairs = [(c, p) for c in range(cps) for p in range(H // 2)]
    ones = jnp.ones((A_ONES, L), BF16)
    low = lax.broadcasted_iota(jnp.int32, (L, 2 * A_DK), 1) < A_DK
    low_row = low[0:1]
    k2 = {(c, p): k_ref[0, span(c), 2 * p * A_DK:2 * (p + 1) * A_DK] for c, p in pairs}
    q2 = {(c, p): q_ref[0, span(c), 2 * p * A_DK:2 * (p + 1) * A_DK] for c, p in pairs}
    q_bd = {i: jnp.concatenate([jnp.where(low, q2[i], 0), jnp.where(low, 0, q2[i])], axis=0).astype(BF16)
            for i in pairs}
    vaug_t = {(c, h): jnp.concatenate([vt_ref[0, h * A_DV:(h + 1) * A_DV, span(c)], ones], axis=0)
              for c, h in items}
    s_pair = {i: _dot_nt(k2[i], q_bd[i]) for i in pairs}
    state_add = {}
    for c, p in pairs:
        col = lambda h: w_c[c][:, lane0 + h:lane0 + h + 1]
        kw = (k2[c, p].astype(F32) * jnp.where(low, col(2 * p), col(2 * p + 1))).astype(BF16)
        state_add[c, p] = (_dot(vaug_t[c, 2 * p], jnp.where(low, kw, 0))
                           + _dot(vaug_t[c, 2 * p + 1], jnp.where(low, 0, kw)))
    p_t = {}
    for c, h in items:
        b_col = b_c[c][:, lane0 + h:lane0 + h + 1]
        w_t = jnp.exp(jnp.where(keep_t, b_col - mx[c][h:h + 1, :], -jnp.inf))
        p_t[c, h] = (s_pair[c, h // 2][:, (h % 2) * L:(h % 2 + 1) * L] * w_t).astype(BF16)
    local_t = {i: _dot(vaug_t[i], p_t[i]) for i in items}

    for p in range(H // 2):
        state = state_ref[p]
        for c in order:
            carry = _dot_nt(state.astype(BF16), q_bd[c, p])
            for h in (2 * p, 2 * p + 1):
                tot = local_t[c, h] + inter[c][h:h + 1, :] * carry[:, (h % 2) * L:(h % 2 + 1) * L]
                den = jnp.maximum(jnp.abs(tot[A_DV:A_DV + 1, :]), floor[c][h:h + 1, :])
                ht_ref[0, h * A_DV:(h + 1) * A_DV, span(c)] = (tot[:A_DV] / den).astype(BF16)
            scale = jnp.where(low_row, decay[c][2 * p:2 * p + 1, :], decay[c][2 * p + 1:2 * p + 2, :])
            state = scale * state + state_add[c, p]
        state_ref[p] = state


def _chunk_order(n_chunks, reverse):
    return (lambda c: n_chunks - 1 - c) if reverse else (lambda c: c)


def _mlstm_scan(q, k, vt, b_c, g_r, bmax_r, carry_f, carry_b):
    bsz, n, _ = q.shape
    cps = min(A_CHUNKS_PER_STEP, n // A_CHUNK)
    L = A_CHUNK * cps
    s_spec = pl.BlockSpec((1,) + A_STATE, lambda b, c: (b, 0, 0, 0))
    m_spec = pl.BlockSpec((1,) + A_MSTATE, lambda b, c: (b, 0, 0))

    def specs(reverse):
        cidx = _chunk_order(n // L, reverse)
        tok = lambda w: pl.BlockSpec((1, L, w), lambda b, c: (b, cidx(c), 0))
        chan = lambda r: pl.BlockSpec((1, r, L), lambda b, c: (b, 0, cidx(c)))
        return ([tok(A_QK), tok(A_QK), chan(A_V), tok(GATE_PAD), chan(2 * A_HEADS), chan(2 * A_HEADS),
                 s_spec, m_spec], [chan(A_V), s_spec, m_spec])

    (in_f, out_f), (in_b, out_b) = specs(False), specs(True)
    shapes = [jax.ShapeDtypeStruct((bsz, A_V, n), BF16), jax.ShapeDtypeStruct((bsz,) + A_STATE, F32),
              jax.ShapeDtypeStruct((bsz,) + A_MSTATE, F32)]
    scratch = [pltpu.VMEM(A_STATE, F32), pltpu.VMEM(A_MSTATE, F32)]
    hf, sf, mf, hb, sb, mb = pl.pallas_call(
        functools.partial(_both_directions(_mlstm_scan_direction, 8, 3, 2), cps=cps),
        grid=(bsz, n // L),
        in_specs=in_f + in_b,
        out_specs=out_f + out_b,
        out_shape=shapes + shapes,
        scratch_shapes=scratch + scratch,
        compiler_params=_cparams(),
        name="mlstm_scan",
    )(q, k, vt, b_c, g_r, bmax_r, *carry_f, q, k, vt, b_c, g_r, bmax_r, *carry_b)
    return hf, hb, (sf, mf), (sb, mb)


def _mlstm_mixer(x_ctx, x_lat, mods, ctx_row, gain, w_in, b_gate, conv_w, conv_b):
    weights = _mlstm_weights(w_in, b_gate, conv_w, conv_b)
    bsz = x_lat.shape[0]
    zero = (jnp.zeros((bsz,) + A_STATE, F32), jnp.zeros((bsz,) + A_MSTATE, F32))
    carry_f, carry_b = zero, zero
    acts = []
    for x, row in ((x_ctx, ctx_row), (x_lat, None)):
        q, k, vt, ot, b_c, g_r, bmax_r = _mlstm_proj(x, mods, row, gain, weights)
        hf, hb, carry_f, carry_b = _mlstm_scan(q, k, vt, b_c, g_r, bmax_r, carry_f, carry_b)
        acts.append((hf, hb, ot))
    return acts[0], acts[1]


def _gla_proj_kernel(x_ref, mod_ref, gain_ref, wqkv_ref, wr_ref, wa_ref, wa2_ref, ba_ref,
                     q_ref, k_ref, v_ref, r_ref, gf_ref, gb_ref, *, rows):
    mod = mod_ref[0]
    h = _norm_mod(x_ref[0], gain_ref[...], _mod_part(mod, 1), _mod_part(mod, 0)).astype(BF16)
    L = B_CHUNK
    a = _dot(h, wa_ref[...]).astype(BF16)
    row = lax.broadcasted_iota(jnp.int32, (L, L), 0)
    col = lax.broadcasted_iota(jnp.int32, (L, L), 1)
    for d, out_ref in enumerate((gf_ref, gb_ref)):
        mask = jnp.where((col >= row) if d else (col <= row), 1.0, 0.0).astype(BF16)
        g = _log_sigmoid(_dot(a, wa2_ref[d]) + ba_ref[d]) / B_TAU
        for c in range(rows // L):
            out_ref[0, c * L:(c + 1) * L, :] = _cumsum_rows(mask, g[c * L:(c + 1) * L])
    qkv = _dot(h, wqkv_ref[...])
    q_ref[0] = qkv[:, :B_QK].astype(BF16)
    k_ref[0] = qkv[:, B_QK:2 * B_QK].astype(BF16)
    v_ref[0] = qkv[:, 2 * B_QK:].astype(BF16)
    r_ref[0] = _dot(h, wr_ref[...]).astype(BF16)


def _gla_proj(x, mods, cond_row, gain, weights):
    bsz, n, _ = x.shape
    rows = _block_rows(n, 1024)
    return pl.pallas_call(
        functools.partial(_gla_proj_kernel, rows=rows),
        grid=(bsz, n // rows),
        in_specs=[_row_spec(rows, D_MODEL), _mod_spec(cond_row), _resident((1, D_MODEL))]
                 + [_resident(w.shape) for w in weights],
        out_specs=[_row_spec(rows, B_QK), _row_spec(rows, B_QK), _row_spec(rows, B_V), _row_spec(rows, B_V),
                   _row_spec(rows, B_QK), _row_spec(rows, B_QK)],
        out_shape=[jax.ShapeDtypeStruct((bsz, n, B_QK), BF16), jax.ShapeDtypeStruct((bsz, n, B_QK), BF16),
                   jax.ShapeDtypeStruct((bsz, n, B_V), BF16), jax.ShapeDtypeStruct((bsz, n, B_V), BF16),
                   jax.ShapeDtypeStruct((bsz, n, B_QK), F32), jax.ShapeDtypeStruct((bsz, n, B_QK), F32)],
        compiler_params=_cparams(),
        name="gla_proj",
    )(x, mods, gain.reshape(1, D_MODEL), *weights)


def _gla_scan_direction(q_ref, k_ref, v_ref, g_ref, s0_ref, o_ref, s_out_ref, state_ref, att_ref,
                        *, reverse, cps, phase):
    L, sub = B_CHUNK, B_SUB
    nsub = L // sub
    step = pl.program_id(1)

    if phase == "init":
        @pl.when(step == 0)
        def _():
            state_ref[...] = s0_ref[0]
        return
    if phase == "final":
        @pl.when(step == pl.num_programs(1) - 1)
        def _():
            s_out_ref[0] = state_ref[...]
        return

    items = [(c, h) for c in range(cps) for h in range(B_HEADS)]
    rows = lambda c: slice(c * L, (c + 1) * L)
    Gs = {(c, h): g_ref[0, rows(c), h * B_DK:(h + 1) * B_DK] for c, h in items}
    qs = {(c, h): q_ref[0, rows(c), h * B_DK:(h + 1) * B_DK].astype(F32) * (B_DK ** -0.5) for c, h in items}
    ks = {(c, h): k_ref[0, rows(c), h * B_DK:(h + 1) * B_DK].astype(F32) for c, h in items}
    vs = {(c, h): v_ref[0, rows(c), h * B_DV:(h + 1) * B_DV] for c, h in items}
    g_tot = {i: (G[0:1] if reverse else G[L - 1:L]) for i, G in Gs.items()}
    q_carry = {i: (qs[i] * jnp.exp(Gs[i])).astype(BF16) for i in items}
    state_add = {i: _dot_tn(vs[i], (ks[i] * jnp.exp(g_tot[i] - Gs[i])).astype(BF16)) for i in items}

    def edge_scores(G, qh, kh, own_block):
        def edge(b):
            first = (b * sub) if own_block != reverse else (b * sub + sub - 1)
            return G[first:first + 1]
        edges = jnp.concatenate([jnp.broadcast_to(edge(b), (sub, B_DK)) for b in range(nsub)], axis=0)
        k_edge = kh * jnp.exp(edges - G)
        q_cols, k_cols = [], []
        for b in range(nsub):
            lo, hi = b * sub, (b + 1) * sub
            if reverse:
                top = lo + sub if own_block else lo
                seen = [qh[:top] * jnp.exp(G[:top] - edge(b)), jnp.zeros((L - top, B_DK), F32)]
            else:
                bot = lo if own_block else hi
                seen = [jnp.zeros((bot, B_DK), F32), qh[bot:] * jnp.exp(G[bot:] - edge(b))]
            q_cols.append(jnp.concatenate([s for s in seen if s.shape[0]], axis=0).astype(BF16))
            k_parts = [jnp.zeros((lo, B_DK), F32), k_edge[lo:hi], jnp.zeros((L - hi, B_DK), F32)]
            k_cols.append(jnp.concatenate([s for s in k_parts if s.shape[0]], axis=0).astype(BF16))
        return _dot_nt(jnp.concatenate(q_cols, axis=1), jnp.concatenate(k_cols, axis=1))

    mild = jnp.min(g_ref[0]) >= -B_MILD_DECAY

    @pl.when(mild)
    def _():
        row = lax.broadcasted_iota(jnp.int32, (L, L), 0)
        col = lax.broadcasted_iota(jnp.int32, (L, L), 1)
        keep = (col >= row) if reverse else (col <= row)
        for n, i in enumerate(items):
            att_ref[n] = jnp.where(keep, edge_scores(Gs[i], qs[i], ks[i], True), 0.0)

    @pl.when(jnp.logical_not(mild))
    def _():
        pair = lax.broadcasted_iota(jnp.int32, (L * sub, 128), 0)
        lane = lax.broadcasted_iota(jnp.int32, (L * sub, 128), 1)
        place = lane == (pair // sub // sub) * sub + pair % sub
        sel = (lax.broadcasted_iota(jnp.int32, (L, L * sub), 1) // sub
               == lax.broadcasted_iota(jnp.int32, (L, L * sub), 0))
        sel_bf16 = jnp.where(sel, 1.0, 0.0).astype(BF16)
        ones_red = jnp.ones((B_DK, 128), BF16)
        srow = lax.broadcasted_iota(jnp.int32, (sub, B_DK), 0)

        def pair_sums(G, qh, kh):
            pieces = []
            for b in range(nsub):
                Gb = G[b * sub:(b + 1) * sub]
                kb = kh[b * sub:(b + 1) * sub]
                for j in range(sub):
                    r = b * sub + j
                    vis = (srow >= j) if reverse else (srow <= j)
                    pieces.append(qh[r:r + 1] * kb * jnp.exp(jnp.where(vis, G[r:r + 1] - Gb, -jnp.inf)))
            return _dot(jnp.concatenate(pieces, axis=0).astype(BF16), ones_red)

        att_cross = {i: edge_scores(Gs[i], qs[i], ks[i], False) for i in items}
        sums = {i: pair_sums(Gs[i], qs[i], ks[i]) for i in items}
        for n, i in enumerate(items):
            local = _dot(sel_bf16, jnp.where(place, sums[i], 0.0).astype(BF16))
            att_ref[n] = att_cross[i] + local[:, :L]

    o_local = {i: _dot(att_ref[n].astype(BF16), vs[i]) for n, i in enumerate(items)}

    order = range(cps - 1, -1, -1) if reverse else range(cps)
    for h in range(B_HEADS):
        state = state_ref[h]
        for c in order:
            o_ref[0, rows(c), h * B_DV:(h + 1) * B_DV] = (
                o_local[c, h] + _dot_nt(q_carry[c, h], state.astype(BF16))).astype(BF16)
            state = state * jnp.exp(g_tot[c, h]) + state_add[c, h]
        state_ref[h] = state


def _gla_scan(q, k, v, g_f, g_b, state_f, state_b):
    bsz, n, _ = q.shape
    cps = min(B_CHUNKS_PER_STEP, n // B_CHUNK)
    L = B_CHUNK * cps
    sshape = (B_HEADS, B_DV, B_DK)
    s_spec = pl.BlockSpec((1,) + sshape, lambda b, c: (b, 0, 0, 0))

    def specs(reverse):
        cidx = _chunk_order(n // L, reverse)
        tok = lambda w: pl.BlockSpec((1, L, w), lambda b, c: (b, cidx(c), 0))
        return [tok(B_QK), tok(B_QK), tok(B_V), tok(B_QK), s_spec], [tok(B_V), s_spec]

    (in_f, out_f), (in_b, out_b) = specs(False), specs(True)
    shapes = [jax.ShapeDtypeStruct((bsz, n, B_V), BF16), jax.ShapeDtypeStruct((bsz,) + sshape, F32)]
    of, sf, ob, sb = pl.pallas_call(
        functools.partial(_both_directions(_gla_scan_direction, 5, 2, 2), cps=cps),
        grid=(bsz, n // L),
        in_specs=in_f + in_b,
        out_specs=out_f + out_b,
        out_shape=shapes + shapes,
        scratch_shapes=[pltpu.VMEM(sshape, F32), pltpu.VMEM((cps * B_HEADS, B_CHUNK, B_CHUNK), F32)] * 2,
        compiler_params=_cparams(),
        name="gla_scan",
    )(q, k, v, g_f, state_f, q, k, v, g_b, state_b)
    return of, ob, sf, sb


def _gla_mixer(x_ctx, x_lat, mods, ctx_row, gain, w_in, w_a2, b_a):
    wqkv = w_in[:, :2 * B_QK + B_V].astype(BF16)
    wr = w_in[:, 2 * B_QK + B_V:2 * B_QK + 2 * B_V].astype(BF16)
    wa = jnp.pad(w_in[:, 2 * B_QK + 2 * B_V:], ((0, 0), (0, GATE_PAD - 2 * B_RANK))).astype(BF16)
    wa2 = jnp.stack([jnp.zeros((GATE_PAD, B_QK), F32).at[d * B_RANK:(d + 1) * B_RANK].set(w_a2[d])
                     for d in range(2)]).astype(BF16)
    ba = b_a.reshape(2, 1, B_QK)
    bsz = x_lat.shape[0]
    state_f = state_b = jnp.zeros((bsz, B_HEADS, B_DV, B_DK), F32)
    acts = []
    for x, row in ((x_ctx, ctx_row), (x_lat, None)):
        q, k, v, r, g_f, g_b = _gla_proj(x, mods, row, gain, (wqkv, wr, wa, wa2, ba))
        of, ob, state_f, state_b = _gla_scan(q, k, v, g_f, g_b, state_f, state_b)
        acts.append((of, ob, r))
    return acts[0], acts[1]


def _na_proj_kernel_ctx(x_ref, mod_ref, gain_ref, w_ref, wvt_ref, q_ref, k_ref, v_ref, vt_ref):
    mod = mod_ref[0]
    h = _norm_mod(x_ref[0], gain_ref[...], _mod_part(mod, 1), _mod_part(mod, 0)).astype(BF16)
    qkv = _dot(h, w_ref[...])
    q_ref[0] = (qkv[:, :D_MODEL] * C_QSCALE).astype(BF16)
    k_ref[0] = qkv[:, D_MODEL:2 * D_MODEL].astype(BF16)
    v_ref[0] = qkv[:, 2 * D_MODEL:].astype(BF16)
    vt_ref[0] = _dot_nt(wvt_ref[...], h).astype(BF16)


def _na_proj_kernel_lat(x_ref, mod_ref, gain_ref, w_ref, wvt_ref, q_ref, k_ref, vt_ref):
    mod = mod_ref[0]
    h = _norm_mod(x_ref[0], gain_ref[...], _mod_part(mod, 1), _mod_part(mod, 0)).astype(BF16)
    qk = _dot(h, w_ref[...])
    q_ref[0] = (qk[:, :D_MODEL] * C_QSCALE).astype(BF16)
    k_ref[0] = qk[:, D_MODEL:].astype(BF16)
    vt_ref[0] = _dot_nt(wvt_ref[...], h).astype(BF16)


def _na_proj(x, mods, cond_row, gain, w_in, *, with_v):
    bsz, n, _ = x.shape
    rows = _block_rows(n, 1024)
    wvt = w_in[:, 2 * D_MODEL:].T.astype(BF16)
    w = (w_in if with_v else w_in[:, :2 * D_MODEL]).astype(BF16)
    tok = jax.ShapeDtypeStruct((bsz, n, D_MODEL), BF16)
    vt_shape = jax.ShapeDtypeStruct((bsz, D_MODEL, n), BF16)
    vt_spec = pl.BlockSpec((1, D_MODEL, rows), lambda b, t: (b, 0, t))
    rs = _row_spec(rows, D_MODEL)
    return pl.pallas_call(
        _na_proj_kernel_ctx if with_v else _na_proj_kernel_lat,
        grid=(bsz, n // rows),
        in_specs=[rs, _mod_spec(cond_row), _resident((1, D_MODEL)), _resident(w.shape), _resident(wvt.shape)],
        out_specs=[rs, rs, rs, vt_spec] if with_v else [rs, rs, vt_spec],
        out_shape=[tok, tok, tok, vt_shape] if with_v else [tok, tok, vt_shape],
        compiler_params=_cparams(),
        name="na_proj_ctx" if with_v else "na_proj_lat",
    )(x, mods, gain.reshape(1, D_MODEL), w, wvt)


def _na_bias_table(rpb):
    colq = np.arange(GRID_W)[None, :]
    colk = np.arange(GRID_W)[:, None]
    start = np.clip(colq - C_WIN_COLS // 2, 0, GRID_W - C_WIN_COLS)
    valid = (colk >= start) & (colk < start + C_WIN_COLS)
    w2 = 2 * GRID_W
    lead = GRID_W - C_WIN_COLS
    padded = jnp.pad(rpb, ((0, 0), (0, 0), (lead, w2 - lead - rpb.shape[-1])))
    skew = jnp.tile(padded, (1, 1, GRID_W))[..., :GRID_W * (w2 - 1)]
    skew = skew.reshape(rpb.shape[0], rpb.shape[1], GRID_W, w2 - 1)[..., GRID_W - 1:]
    bias = jnp.where(valid[None, None], jnp.swapaxes(skew, 2, 3) * LOG2E, -jnp.inf)
    bias = bias.transpose(1, 2, 0, 3).reshape(2 * C_WIN_ROWS - 1, GRID_W, C_HEADS * GRID_W)
    return jnp.concatenate([jnp.full((1,) + bias.shape[1:], -jnp.inf, F32), bias], axis=0)


def _na_lat_kernel(q_ref, k_ref, kc_ref, vt_ref, vtc_ref, tab_ref, ot_ref, *, rows):
    r0 = 2 * pl.program_id(1)
    gw = C_GROUP * C_DH
    nk = C_KROWS * GRID_W
    first = lambda r: jnp.clip(r - C_WIN_ROWS // 2, 0, rows - C_WIN_ROWS)
    rs_load = jnp.minimum(first(r0) - first(r0) % 2, rows - C_KROWS)
    start = pl.multiple_of(rs_load * GRID_W, 2 * GRID_W)
    entries = []
    for i in range(C_KROWS):
        kr = rs_load + i
        per_q = []
        for r in (r0, r0 + 1):
            ok = jnp.logical_and(kr >= first(r), kr < first(r) + C_WIN_ROWS)
            per_q.append(jnp.where(ok, kr - r + C_WIN_ROWS, 0))
        entries.append(per_q)

    blk_r = lax.broadcasted_iota(jnp.int32, (gw, gw), 0) // C_DH
    blk_c = lax.broadcasted_iota(jnp.int32, (gw, gw), 1) // C_DH
    same_head = blk_r == blk_c

    def block_diag(qg):
        return jnp.where(same_head, jnp.concatenate([qg] * C_GROUP, axis=0), 0).astype(BF16)

    n_groups = C_HEADS // C_GROUP
    gsl = [slice(g * gw, (g + 1) * gw) for g in range(n_groups)]
    for w0 in range(0, n_groups, C_WAVE):
        wave = range(w0, min(w0 + C_WAVE, n_groups))
        q_bd = {g: jnp.concatenate([block_diag(q_ref[0, :GRID_W, gsl[g]]),
                                    block_diag(q_ref[0, GRID_W:, gsl[g]])], axis=0)
                for g in wave}
        s_loc = {g: _dot_nt(k_ref[0, pl.ds(start, nk), gsl[g]], q_bd[g]) for g in wave}
        s_ctx = {g: _dot_nt(kc_ref[0, :, gsl[g]], q_bd[g]) for g in wave}
        p_loc, p_ctx, denom = {}, {}, {}
        for g in wave:
            bias = jnp.concatenate(
                [jnp.concatenate([tab_ref[e, :, gsl[g]] for e in per_q], axis=1) for per_q in entries], axis=0)
            sl = s_loc[g] + bias
            m = jnp.maximum(jnp.max(sl, axis=0, keepdims=True), jnp.max(s_ctx[g], axis=0, keepdims=True))
            pl_, pc_ = jnp.exp2(sl - m), jnp.exp2(s_ctx[g] - m)
            denom[g] = jnp.sum(pl_, axis=0, keepdims=True) + jnp.sum(pc_, axis=0, keepdims=True)
            p_loc[g] = pl_.astype(BF16)
            p_ctx[g] = pc_.astype(BF16)
        for g in wave:
            ot = (_dot(vt_ref[0, gsl[g], pl.ds(start, nk)], p_loc[g])
                  + _dot(vtc_ref[0, gsl[g], :], p_ctx[g])) / denom[g]
            for qr in range(2):
                for h in range(C_GROUP):
                    ch = slice(g * gw + h * C_DH, g * gw + (h + 1) * C_DH)
                    ot_ref[0, ch, qr * GRID_W:(qr + 1) * GRID_W] = (
                        ot[h * C_DH:(h + 1) * C_DH, qr * gw + h * C_DH:qr * gw + (h + 1) * C_DH].astype(BF16))


def _na_lat_attention(q, k, kc, vt, vtc, tab):
    bsz, n, _ = q.shape
    rows = n // GRID_W
    n_ctx = kc.shape[1]
    assert rows >= C_KROWS and rows % 2 == 0
    return pl.pallas_call(
        functools.partial(_na_lat_kernel, rows=rows),
        grid=(bsz, rows // 2),
        in_specs=[pl.BlockSpec((1, 2 * GRID_W, D_MODEL), lambda b, r: (b, r, 0)),
                  pl.BlockSpec((1, n, D_MODEL), lambda b, r: (b, 0, 0)),
                  pl.BlockSpec((1, n_ctx, D_MODEL), lambda b, r: (b, 0, 0)),
                  pl.BlockSpec((1, D_MODEL, n), lambda b, r: (b, 0, 0)),
                  pl.BlockSpec((1, D_MODEL, n_ctx), lambda b, r: (b, 0, 0)),
                  _resident(tab.shape)],
        out_specs=pl.BlockSpec((1, D_MODEL, 2 * GRID_W), lambda b, r: (b, 0, r)),
        out_shape=jax.ShapeDtypeStruct((bsz, D_MODEL, n), BF16),
        compiler_params=_cparams(),
        name="na_lat_attention",
    )(q, k, kc, vt, vtc, tab)


def _na_ctx_kernel(q_ref, k_ref, v_ref, o_ref):
    hsl = [slice(h * C_DH, (h + 1) * C_DH) for h in range(C_HEADS)]
    s = [_dot_nt(q_ref[0, :, hs], k_ref[0, :, hs]) for hs in hsl]
    p = [jnp.exp2(sh - jnp.max(sh, axis=-1, keepdims=True)) for sh in s]
    o = [_dot(ph.astype(BF16), v_ref[0, :, hs]) for ph, hs in zip(p, hsl)]
    for oh, ph, hs in zip(o, p, hsl):
        o_ref[0, :, hs] = (oh / jnp.sum(ph, axis=-1, keepdims=True)).astype(BF16)


def _na_ctx_attention(q, k, v):
    bsz, n, _ = q.shape
    spec = pl.BlockSpec((1, n, D_MODEL), lambda b: (b, 0, 0))
    return pl.pallas_call(
        _na_ctx_kernel,
        grid=(bsz,),
        in_specs=[spec, spec, spec],
        out_specs=spec,
        out_shape=jax.ShapeDtypeStruct((bsz, n, D_MODEL), BF16),
        compiler_params=_cparams(1),
        name="na_ctx_attention",
    )(q, k, v)


def kernel(x, c, ctx, c_ctx, w_ada, b_ada, norm_mix, norm_ffn, w_up, ffn_conv_w, ffn_conv_b, w_down,
           a_w_in, a_b_gate, a_conv_w, a_conv_b, a_w_out,
           b_w_in, b_w_a2, b_b_a, b_norm, b_w_out,
           c_w_in, c_rpb, c_w_out, norm_final):
    bsz = x.shape[0]
    depth = w_ada.shape[0]
    ctx_row = bsz
    assert bsz < COND_ROWS

    cond = jnp.concatenate([c, c_ctx[None], jnp.zeros((COND_ROWS - bsz - 1, D_MODEL), F32)], axis=0)
    mod_all = _ada_modulation(cond, w_ada, b_ada)
    x_ctx, x_lat = ctx, x

    for i in range(depth):
        last = i == depth - 1
        mods = mod_all[i].reshape(COND_ROWS, 1, 6 * D_MODEL)
        kind, j = i % 3, i // 3
        if kind == 0:
            acts_c, acts_l = _mlstm_mixer(x_ctx, x_lat, mods, ctx_row, norm_mix[i], a_w_in[j], a_b_gate[j],
                                          a_conv_w[j], a_conv_b[j])
            out = functools.partial(_mixer_out, _mlstm_out_kernel, "mlstm_out", consts=[], mods=mods,
                                    w_out=a_w_out[j], act_specs="ttt")
        elif kind == 1:
            acts_c, acts_l = _gla_mixer(x_ctx, x_lat, mods, ctx_row, norm_mix[i], b_w_in[j], b_w_a2[j], b_b_a[j])
            out = functools.partial(_mixer_out, _gla_out_kernel, "gla_out", consts=[b_norm[j].reshape(1, B_DV)],
                                    mods=mods, w_out=b_w_out[j], act_specs="rrr")
        else:
            qc, kc, vc, vtc = _na_proj(x_ctx, mods, ctx_row, norm_mix[i], c_w_in[j], with_v=True)
            ql, kl, vtl = _na_proj(x_lat, mods, None, norm_mix[i], c_w_in[j], with_v=False)
            ot = _na_lat_attention(ql, kl, kc, vtl, vtc, _na_bias_table(c_rpb[j]))
            mid_lat = _mixer_out(_na_out_t_kernel, "na_out_lat", x_lat, "t", [ot], [], mods, None, c_w_out[j])
            if not last:
                oc = _na_ctx_attention(qc, kc, vc)
                mid_ctx = _mixer_out(_na_out_kernel, "na_out_ctx", x_ctx, "r", [oc], [], mods, ctx_row, c_w_out[j])
        if kind != 2:
            mid_lat = out(x=x_lat, acts=list(acts_l), cond_row=None)
            if not last:
                mid_ctx = out(x=x_ctx, acts=list(acts_c), cond_row=ctx_row)
        ffn_w = _ffn_weights(w_up[i], ffn_conv_w[i], ffn_conv_b[i], w_down[i])
        x_lat = _conv_ffn(mid_lat, mods, None, norm_ffn[i], ffn_w, norm_final, final=last)
        if not last:
            x_ctx = _conv_ffn(mid_ctx, mods, ctx_row, norm_ffn[i], ffn_w, norm_final, final=False)
    return x_lat
```
